```python
import jax, jax.numpy as jnp
from jax import lax
import numpy as np

D_MODEL = 1024
BATCH = 1
SEQ = 16384
DEPTH = 4

GRID_W = 64
CTX_LEN = 256
N_MIXERS = 4
GROUP_W = D_MODEL // N_MIXERS
HEAD_DIM = 64
N_GROUP_HEADS = GROUP_W // HEAD_DIM
NA_KH = 8
NA_KW = 16
GQA_KV_HEADS = N_GROUP_HEADS // 2
MLA_Q_LORA = GROUP_W
MLA_KV_LORA = GROUP_W // 2
MLA_NOPE = HEAD_DIM
MLA_ROPE = HEAD_DIM // 2
MLA_V = HEAD_DIM
LRU_BLOCKS = N_GROUP_HEADS
LRU_BW = GROUP_W // LRU_BLOCKS
LRU_CONV = 4
LRU_C = 8.0
N_EXPERTS = 32
TOP_K = 4
D_EXPERT = D_MODEL
SWIGLU_LIMIT = 7.0
SWIGLU_ALPHA = 1.702
EXPERT_BLOCK = 128
Q_BLOCK = 128
ROPE_THETA = 10000.0
EPS = 1e-6
IN_SIZES = (GROUP_W, GROUP_W, GROUP_W,
            GROUP_W, GQA_KV_HEADS * HEAD_DIM, GQA_KV_HEADS * HEAD_DIM,
            MLA_Q_LORA, MLA_KV_LORA, MLA_ROPE,
            GROUP_W, GROUP_W)
IN_COLS = sum(IN_SIZES)

kernel_name = "hybrid_parallel_heads_dit_moe"


def rmsnorm(x, g):
    xf = x.astype(jnp.float32)
    y = xf * lax.rsqrt(jnp.mean(xf * xf, axis=-1, keepdims=True) + EPS)
    return y.astype(x.dtype) * g


def heads(x, n):
    B, N, _ = x.shape
    return x.reshape(B, N, n, -1).transpose(0, 2, 1, 3)


def merge_heads(x):
    B, H, N, d = x.shape
    return x.transpose(0, 2, 1, 3).reshape(B, N, H * d)


def split_cols(u):
    offs, o = [], 0
    for s in IN_SIZES[:-1]:
        o += s
        offs.append(o)
    return jnp.split(u, offs, axis=-1)


def grid_angles(n_tok, rot_dim):
    t = jnp.arange(n_tok)
    row = (t // GRID_W).astype(jnp.float32)
    col = (t % GRID_W).astype(jnp.float32)
    ax = rot_dim // 2
    inv = ROPE_THETA ** (-jnp.arange(0, ax, 2, dtype=jnp.float32) / ax)
    return jnp.concatenate([row[:, None] * inv, col[:, None] * inv], axis=-1)


def rope_2d(x, ang):
    h = x.shape[-1] // 4
    cos = jnp.cos(ang).astype(x.dtype).reshape(-1, 2, h)
    sin = jnp.sin(ang).astype(x.dtype).reshape(-1, 2, h)
    xr = x.reshape(*x.shape[:-1], 2, 2, h)
    x1, x2 = xr[..., 0, :], xr[..., 1, :]
    out = jnp.stack([x1 * cos - x2 * sin, x2 * cos + x1 * sin], axis=-2)
    return out.reshape(x.shape)


def blocked_attention(q, k, v, scale):
    B, KH, G, N, d = q.shape
    qb = min(Q_BLOCK, N)
    qs = q.reshape(B, KH, G, N // qb, qb, d).transpose(3, 0, 1, 2, 4, 5)

    def one(qblk):
        s = jnp.einsum('bkgqd,bkmd->bkgqm', qblk, k).astype(jnp.float32) * scale
        p = jax.nn.softmax(s, axis=-1).astype(v.dtype)
        return jnp.einsum('bkgqm,bkmv->bkgqv', p, v)

    o = lax.map(one, qs)
    return o.transpose(1, 2, 3, 0, 4, 5).reshape(B, KH, G, N, v.shape[-1])


def neighbourhood_attention(q, k, v, kc, vc, rpb, rows, scale):
    B, H, S, d = q.shape
    kh = min(NA_KH, rows)
    t = jnp.arange(S)
    r, cidx = t // GRID_W, t % GRID_W
    r0 = jnp.clip(r - kh // 2, 0, rows - kh)
    c0 = jnp.clip(cidx - NA_KW // 2, 0, GRID_W - NA_KW)
    kr = r0[:, None] + jnp.arange(kh)[None, :]
    kcol = c0[:, None] + jnp.arange(NA_KW)[None, :]
    nbr = (kr[:, :, None] * GRID_W + kcol[:, None, :]).reshape(S, kh * NA_KW)
    rel = ((kr - r[:, None] + NA_KH - 1)[:, :, None] * (2 * NA_KW - 1)
           + (kcol - cidx[:, None] + NA_KW - 1)[:, None, :]).reshape(S, kh * NA_KW)
    rpb_flat = rpb.reshape(H, -1)
    nb = S // Q_BLOCK
    n_key = kh * NA_KW
    qs = q.reshape(B, H, nb, Q_BLOCK, d).transpose(2, 0, 1, 3, 4)

    def one(args):
        qblk, idx, ridx = args
        kn = k[:, :, idx]
        vn = v[:, :, idx]
        s_nb = (jnp.einsum('bhqd,bhqkd->bhqk', qblk, kn).astype(jnp.float32) * scale
                + rpb_flat[:, ridx].astype(jnp.float32))
        s_cx = jnp.einsum('bhqd,bhcd->bhqc', qblk, kc).astype(jnp.float32) * scale
        p = jax.nn.softmax(jnp.concatenate([s_nb, s_cx], axis=-1), axis=-1).astype(v.dtype)
        return (jnp.einsum('bhqk,bhqkd->bhqd', p[..., :n_key], vn)
                + jnp.einsum('bhqc,bhcd->bhqd', p[..., n_key:], vc))

    o = lax.map(one, (qs, nbr.reshape(nb, Q_BLOCK, n_key), rel.reshape(nb, Q_BLOCK, n_key)))
    return o.transpose(1, 2, 0, 3, 4).reshape(B, H, S, d)


def mla_q(cq, qa_g, wuq, qn, ang):
    q = heads(rmsnorm(cq, qa_g) @ wuq, N_GROUP_HEADS)
    q_nope = rmsnorm(q[..., :MLA_NOPE], qn[:MLA_NOPE])
    q_rope = rmsnorm(q[..., MLA_NOPE:], qn[MLA_NOPE:])
    if ang is not None:
        q_rope = rope_2d(q_rope, ang)
    return jnp.concatenate([q_nope, q_rope], axis=-1)


def mla_kv(ckv, kr, kva_g, wukv, kn, ang):
    kv = heads(rmsnorm(ckv, kva_g) @ wukv, N_GROUP_HEADS)
    k_nope = rmsnorm(kv[..., :MLA_NOPE], kn[:MLA_NOPE])
    v = kv[..., MLA_NOPE:]
    k_rope = rmsnorm(kr, kn[MLA_NOPE:])[:, None]
    if ang is not None:
        k_rope = rope_2d(k_rope, ang)
    k_rope = jnp.broadcast_to(k_rope, k_nope.shape[:-1] + (MLA_ROPE,))
    return jnp.concatenate([k_nope, k_rope], axis=-1), v


def conv_centred(x, w, b):
    y = lax.conv_general_dilated(x, w, window_strides=(1,),
                                 padding=[((LRU_CONV - 1) // 2, LRU_CONV // 2)],
                                 dimension_numbers=('NWC', 'WIO', 'NWC'),
                                 feature_group_count=x.shape[-1])
    return y + b


def rglru_coeffs(xb, wa, ba, wi, bi, lam):
    B, N, W = xb.shape
    xg = xb.reshape(B, N, LRU_BLOCKS, LRU_BW)
    r = jax.nn.sigmoid(jnp.einsum('bngi,gij->bngj', xg, wa).reshape(B, N, W) + ba)
    i = jax.nn.sigmoid(jnp.einsum('bngi,gij->bngj', xg, wi).reshape(B, N, W) + bi)
    log_a = -LRU_C * r.astype(jnp.float32) * jax.nn.softplus(-lam.astype(jnp.float32))
    a = jnp.exp(log_a)
    mult = jnp.sqrt(-jnp.expm1(2.0 * log_a))
    return a, mult * (i * xb).astype(jnp.float32)


def linear_scan(a, b, h0):
    b = b.at[:, 0].add(a[:, 0] * h0)

    def comb(e1, e2):
        a1, b1 = e1
        a2, b2 = e2
        return a1 * a2, a2 * b1 + b2

    _, h = lax.associative_scan(comb, (a, b), axis=1)
    return h


def rglru_bidirectional(xl, xc, wa, ba, wi, bi, lam, with_ctx):
    B, S, W = xl.shape
    outs_l, outs_c = [], []
    for d in range(2):
        rev = (lambda z: jnp.flip(z, axis=1)) if d else (lambda z: z)
        a_c, b_c = rglru_coeffs(rev(xc), wa[d], ba[d], wi[d], bi[d], lam[d])
        h_c = linear_scan(a_c, b_c, jnp.zeros((B, W), jnp.float32))
        a_l, b_l = rglru_coeffs(rev(xl), wa[d], ba[d], wi[d], bi[d], lam[d])
        h_l = linear_scan(a_l, b_l, h_c[:, -1])
        outs_l.append(rev(h_l))
        outs_c.append(rev(h_c))
    y_l = (outs_l[0] + outs_l[1]).astype(xl.dtype)
    y_c = (outs_c[0] + outs_c[1]).astype(xc.dtype) if with_ctx else None
    return y_l, y_c


def merge_groups(parts, grp_g, w_out):
    y = jnp.concatenate(parts, axis=-1)
    B, N, _ = y.shape
    y = rmsnorm(y.reshape(B, N, N_MIXERS, GROUP_W), grp_g.reshape(N_MIXERS, GROUP_W))
    return y.reshape(B, N, D_MODEL) @ w_out


def token_mixers(h, hc, w_in, na_qn, na_kn, na_rpb, gqa_qn, gqa_kn,
                 mla_qa_g, mla_kva_g, mla_wuq, mla_wukv, mla_qn, mla_kn,
                 lru_conv_w, lru_conv_b, lru_wa, lru_ba, lru_wi, lru_bi, lru_lam,
                 grp_g, w_out, with_ctx):
    B, S, _ = h.shape
    rows = S // GRID_W
    H, KVH = N_GROUP_HEADS, GQA_KV_HEADS
    sc = HEAD_DIM ** -0.5
    na_q, na_k, na_v, g_q, g_k, g_v, m_cq, m_ckv, m_kr, l_x, l_gate = split_cols(h @ w_in)
    na_qc, na_kc, na_vc, g_qc, g_kc, g_vc, m_cqc, m_ckvc, m_krc, l_xc, l_gatec = split_cols(hc @ w_in)

    kac = rmsnorm(heads(na_kc, H), na_kn)
    vac = heads(na_vc, H)
    o_a = neighbourhood_attention(rmsnorm(heads(na_q, H), na_qn), rmsnorm(heads(na_k, H), na_kn),
                                  heads(na_v, H), kac, vac, na_rpb, rows, sc)

    ang = grid_angles(S, HEAD_DIM)
    qb = rope_2d(rmsnorm(heads(g_q, H), gqa_qn), ang)
    kbc = rmsnorm(heads(g_kc, KVH), gqa_kn)
    vbc = heads(g_vc, KVH)
    kb_all = jnp.concatenate([rope_2d(rmsnorm(heads(g_k, KVH), gqa_kn), ang), kbc], axis=2)
    vb_all = jnp.concatenate([heads(g_v, KVH), vbc], axis=2)
    o_b = blocked_attention(qb.reshape(B, KVH, H // KVH, S, HEAD_DIM), kb_all, vb_all, sc)
    o_b = o_b.reshape(B, H, S, HEAD_DIM)

    ang_m = grid_angles(S, MLA_ROPE)
    sc_m = (MLA_NOPE + MLA_ROPE) ** -0.5
    qm = mla_q(m_cq, mla_qa_g, mla_wuq, mla_qn, ang_m)
    km, vm = mla_kv(m_ckv, m_kr, mla_kva_g, mla_wukv, mla_kn, ang_m)
    kmc, vmc = mla_kv(m_ckvc, m_krc, mla_kva_g, mla_wukv, mla_kn, None)
    o_c = blocked_attention(qm[:, :, None], jnp.concatenate([km, kmc], axis=2),
                            jnp.concatenate([vm, vmc], axis=2), sc_m).reshape(B, H, S, MLA_V)

    y_l, y_c = rglru_bidirectional(conv_centred(l_x, lru_conv_w, lru_conv_b),
                                   conv_centred(l_xc, lru_conv_w, lru_conv_b),
                                   lru_wa, lru_ba, lru_wi, lru_bi, lru_lam, with_ctx)
    o_d = y_l * jax.nn.gelu(l_gate)

    y = merge_groups([merge_heads(o_a), merge_heads(o_b), merge_heads(o_c), o_d], grp_g, w_out)
    if not with_ctx:
        return y, None
    C = hc.shape[1]
    oac = blocked_attention(rmsnorm(heads(na_qc, H), na_qn)[:, :, None], kac, vac, sc).reshape(B, H, C, HEAD_DIM)
    obc = blocked_attention(rmsnorm(heads(g_qc, H), gqa_qn).reshape(B, KVH, H // KVH, C, HEAD_DIM),
                            kbc, vbc, sc).reshape(B, H, C, HEAD_DIM)
    qmc = mla_q(m_cqc, mla_qa_g, mla_wuq, mla_qn, None)
    occ = blocked_attention(qmc[:, :, None], kmc, vmc, sc_m).reshape(B, H, C, MLA_V)
    odc = y_c * jax.nn.gelu(l_gatec)
    yc = merge_groups([merge_heads(oac), merge_heads(obc), merge_heads(occ), odc], grp_g, w_out)
    return y, yc


def moe_ffn(h, router_w, router_b, w_gu, b_gu, w_down, b_down):
    T, D = h.shape
    logits = (h @ router_w + router_b).astype(jnp.float32)
    top_val, top_idx = lax.top_k(logits, TOP_K)
    gates = jax.nn.softmax(top_val, axis=-1)
    M = T * TOP_K
    e_flat = top_idx.reshape(M)
    tok_flat = jnp.arange(M, dtype=jnp.int32) // TOP_K
    order = jnp.argsort(e_flat)
    e_s, tok_s = e_flat[order], tok_flat[order]
    g_s = gates.reshape(M)[order].astype(h.dtype)
    counts = jnp.bincount(e_flat, length=N_EXPERTS)
    padded = (counts + EXPERT_BLOCK - 1) // EXPERT_BLOCK * EXPERT_BLOCK
    pad_end = jnp.cumsum(padded)
    pad_start = pad_end - padded
    start = jnp.cumsum(counts) - counts
    pos = pad_start[e_s] + jnp.arange(M, dtype=jnp.int32) - start[e_s]
    n_blocks = -(-(M + N_EXPERTS * (EXPERT_BLOCK - 1)) // EXPERT_BLOCK)
    P = n_blocks * EXPERT_BLOCK
    row_tok = jnp.full((P,), T, jnp.int32).at[pos].set(tok_s)
    row_gate = jnp.zeros((P,), h.dtype).at[pos].set(g_s)
    blk_expert = jnp.minimum(jnp.searchsorted(pad_end, jnp.arange(n_blocks) * EXPERT_BLOCK, side='right'),
                             N_EXPERTS - 1)
    h_pad = jnp.concatenate([h, jnp.zeros((1, D), h.dtype)], axis=0)
    xin = h_pad[row_tok].reshape(n_blocks, EXPERT_BLOCK, D)

    def expert_block(args):
        xb, e = args
        gu = xb @ w_gu[e] + b_gu[e]
        x_glu = jnp.minimum(gu[:, :D_EXPERT], SWIGLU_LIMIT)
        x_lin = jnp.clip(gu[:, D_EXPERT:], -SWIGLU_LIMIT, SWIGLU_LIMIT)
        act = x_glu * jax.nn.sigmoid(SWIGLU_ALPHA * x_glu) * (x_lin + 1)
        return act @ w_down[e] + b_down[e]

    y = lax.map(expert_block, (xin, blk_expert)).reshape(P, D) * row_gate[:, None]
    return jax.ops.segment_sum(y, row_tok, num_segments=T + 1)[:T]


def setup_inputs(seed: int = 0) -> dict:
    key = jax.random.key(seed)
    ks = jax.random.split(key, 36)

    def nrm(i, shape, scale):
        return jax.random.normal(ks[i], shape, jnp.float32) * scale

    def gain(i, shape):
        return 1.0 + 0.1 * jax.random.normal(ks[i], shape, jnp.float32)

    L, D, H = DEPTH, D_MODEL, N_GROUP_HEADS
    s = D ** -0.5
    a_base = jax.random.uniform(ks[26], (L, 2, GROUP_W), jnp.float32, 0.9, 0.999) ** (1.0 / LRU_C)
    return {
        "x": nrm(0, (BATCH, SEQ, D), 1.0),
        "c": nrm(1, (BATCH, D), 1.0),
        "ctx": nrm(2, (BATCH, CTX_LEN, D), 1.0),
        "c_ctx": nrm(3, (D,), 1.0),
        "ada_w": nrm(4, (L, D, 6 * D), 0.5 * s),
        "ada_b": nrm(5, (L, 6 * D), 0.02),
        "norm1_g": gain(6, (L, D)),
        "norm2_g": gain(7, (L, D)),
        "w_in": nrm(8, (L, D, IN_COLS), s),
        "na_qn": gain(9, (L, HEAD_DIM)),
        "na_kn": gain(10, (L, HEAD_DIM)),
        "na_rpb": nrm(11, (L, H, 2 * NA_KH - 1, 2 * NA_KW - 1), 0.5),
        "gqa_qn": gain(12, (L, HEAD_DIM)),
        "gqa_kn": gain(13, (L, HEAD_DIM)),
        "mla_qa_g": gain(14, (L, MLA_Q_LORA)),
        "mla_kva_g": gain(15, (L, MLA_KV_LORA)),
        "mla_wuq": nrm(16, (L, MLA_Q_LORA, H * (MLA_NOPE + MLA_ROPE)), MLA_Q_LORA ** -0.5),
        "mla_wukv": nrm(17, (L, MLA_KV_LORA, H * (MLA_NOPE + MLA_V)), MLA_KV_LORA ** -0.5),
        "mla_qn": gain(18, (L, MLA_NOPE + MLA_ROPE)),
        "mla_kn": gain(19, (L, MLA_NOPE + MLA_ROPE)),
        "lru_conv_w": nrm(20, (L, LRU_CONV, 1, GROUP_W), LRU_CONV ** -0.5),
        "lru_conv_b": nrm(21, (L, GROUP_W), 0.02),
        "lru_wa": nrm(22, (L, 2, LRU_BLOCKS, LRU_BW, LRU_BW), LRU_BW ** -0.5),
        "lru_ba": nrm(23, (L, 2, GROUP_W), 0.02),
        "lru_wi": nrm(24, (L, 2, LRU_BLOCKS, LRU_BW, LRU_BW), LRU_BW ** -0.5),
        "lru_bi": nrm(25, (L, 2, GROUP_W), 0.02),
        "lru_lam": jnp.log(a_base) - jnp.log1p(-a_base),
        "grp_g": gain(27, (L, D)),
        "w_out": nrm(28, (L, D, D), s),
        "router_w": nrm(29, (L, D, N_EXPERTS), s),
        "router_b": nrm(30, (L, N_EXPERTS), 0.01),
        "exp_w_gu": nrm(31, (L, N_EXPERTS, D, 2 * D_EXPERT), s),
        "exp_b_gu": nrm(32, (L, N_EXPERTS, 2 * D_EXPERT), 0.02),
        "exp_w_down": nrm(33, (L, N_EXPERTS, D_EXPERT, D), D_EXPERT ** -0.5),
        "exp_b_down": nrm(34, (L, N_EXPERTS, D), 0.02),
    }


def reference(x, c, ctx, c_ctx, ada_w, ada_b, norm1_g, norm2_g, w_in, na_qn, na_kn, na_rpb,
              gqa_qn, gqa_kn, mla_qa_g, mla_kva_g, mla_wuq, mla_wukv, mla_qn, mla_kn,
              lru_conv_w, lru_conv_b, lru_wa, lru_ba, lru_wi, lru_bi, lru_lam, grp_g, w_out,
              router_w, router_b, exp_w_gu, exp_b_gu, exp_w_down, exp_b_down):
    B, S, D = x.shape
    xc = ctx
    for l in range(DEPTH):
        with_ctx = l < DEPTH - 1
        mod = (jax.nn.silu(c) @ ada_w[l] + ada_b[l])[:, None, :]
        mod_c = (jax.nn.silu(c_ctx) @ ada_w[l] + ada_b[l])[None, None, :]
        sh1, sc1, gt1, sh2, sc2, gt2 = jnp.split(mod, 6, axis=-1)
        csh1, csc1, cgt1, csh2, csc2, cgt2 = jnp.split(mod_c, 6, axis=-1)
        h = rmsnorm(x, norm1_g[l]) * (1 + sc1) + sh1
        hc = rmsnorm(xc, norm1_g[l]) * (1 + csc1) + csh1
        y, yc = token_mixers(h, hc, w_in[l], na_qn[l], na_kn[l], na_rpb[l], gqa_qn[l], gqa_kn[l],
                             mla_qa_g[l], mla_kva_g[l], mla_wuq[l], mla_wukv[l], mla_qn[l], mla_kn[l],
                             lru_conv_w[l], lru_conv_b[l], lru_wa[l], lru_ba[l], lru_wi[l], lru_bi[l],
                             lru_lam[l], grp_g[l], w_out[l], with_ctx)
        x = x + gt1 * y
        tokens = (rmsnorm(x, norm2_g[l]) * (1 + sc2) + sh2).reshape(B * S, D)
        if with_ctx:
            xc = xc + cgt1 * yc
            h2c = rmsnorm(xc, norm2_g[l]) * (1 + csc2) + csh2
            tokens = jnp.concatenate([tokens, h2c.reshape(-1, D)], axis=0)
        f = moe_ffn(tokens, router_w[l], router_b[l], exp_w_gu[l], exp_b_gu[l], exp_w_down[l], exp_b_down[l])
        x = x + gt2 * f[:B * S].reshape(B, S, D)
        if with_ctx:
            xc = xc + cgt2 * f[B * S:].reshape(xc.shape)
    return x
```

```python
import functools

import numpy as np
import jax
import jax.numpy as jnp
from jax import lax
from jax.experimental import pallas as pl
from jax.experimental.pallas import tpu as pltpu

F32 = jnp.float32
BF16 = jnp.bfloat16
HIGHEST = lax.Precision.HIGHEST

GRID_W = 64
HEAD_DIM = 64
N_HEADS = 4
GROUP_W = 256
GQA_KV_HEADS = 2
NA_KH = 8
NA_KW = 16
MLA_NOPE = 64
MLA_ROPE = 32
MLA_PAD = 128
LRU_C = 8.0
N_EXPERTS = 32
TOP_K = 4
SWIGLU_LIMIT = 7.0
SWIGLU_ALPHA = 1.702
ROPE_THETA = 10000.0
EPS = 1e-6
NEG = -1e30
LANES = 128
VMEM_LIMIT = 56 * 1024 * 1024

_C_NAQ, _C_NAK, _C_NAV = 0, 256, 512
_C_GQ, _C_GK, _C_GV = 768, 1024, 1152
_C_MCQ, _C_MCKV = 1280, 1536
_C_LX, _C_LG = 1664, 1920
_C_MKR = 2176
IN_COLS_R = 2304


def _params(n_axes, vmem=VMEM_LIMIT):
    return pltpu.CompilerParams(dimension_semantics=("arbitrary",) * n_axes,
                                vmem_limit_bytes=vmem)


def _full(shape):
    n = len(shape)
    return pl.BlockSpec(shape, lambda *_: (0,) * n)


def _mod_kernel(c_ref, w_ref, b_ref, o_ref):
    cc = c_ref[...]
    s = cc * jax.nn.sigmoid(cc)
    o_ref[0] = jnp.dot(s, w_ref[0], preferred_element_type=F32, precision=HIGHEST) + b_ref[0]


def _modulation(c8, ada_w, ada_b):
    L, D, D6 = ada_w.shape
    tn = D6 // 6
    return pl.pallas_call(
        _mod_kernel,
        grid=(L, D6 // tn),
        in_specs=[pl.BlockSpec((8, D), lambda l, j: (0, 0)),
                  pl.BlockSpec((1, D, tn), lambda l, j: (l, 0, j)),
                  pl.BlockSpec((1, 1, tn), lambda l, j: (l, 0, j))],
        out_specs=pl.BlockSpec((1, 8, tn), lambda l, j: (l, 0, j)),
        out_shape=jax.ShapeDtypeStruct((L, 8, D6), F32),
        compiler_params=_params(2),
        name="adaln_mod",
    )(c8, ada_w, ada_b.reshape(L, 1, D6))


def _rms(z):
    return z * lax.rsqrt(jnp.mean(z * z, axis=-1, keepdims=True) + EPS)


def _group_rms(z, gmat, inv_n):
    zz = z * z
    hi = zz.astype(BF16)
    lo = (zz - hi.astype(F32)).astype(BF16)
    ss = (jnp.dot(hi, gmat, preferred_element_type=F32)
          + jnp.dot(lo, gmat, preferred_element_type=F32))
    return z * lax.rsqrt(ss * inv_n + EPS)


def _rope(z, cos, sin_lo, sin_hi, half):
    w = z.shape[-1]
    return z * cos + pltpu.roll(z, half, 1) * sin_lo + pltpu.roll(z, w - half, 1) * sin_hi


def _tile4(t):
    return jnp.concatenate([t, t, t, t], axis=-1)


def _premix_kernel(x_ref, mod_ref, g1_ref, win_ref, bd_ref, gm_ref,
                   cq_ref, slq_ref, shq_ref, cm_ref, slm_ref, shm_ref,
                   naqn_ref, nakn_ref, gqn_ref, gkn_ref, qag_ref, kvag_ref,
                   mqn_ref, mknn_ref, mknr_ref, invn_ref, wuq_ref, wukvk_ref, wukvv_ref,
                   naq_o, nak_o, nav_o, gq_o, gk_o, gv_o, mq_o, mk_o, mv_o, lx_o, lg_o):
    tm = x_ref.shape[0]
    xt = x_ref[...]
    sh1 = mod_ref[0, 0:1, :]
    sc1 = mod_ref[0, 1:2, :]
    h = _rms(xt) * g1_ref[...] * (1.0 + sc1) + sh1
    u = jnp.dot(h.astype(BF16), win_ref[...], preferred_element_type=F32)

    bd = bd_ref[...]
    inv64 = 1.0 / HEAD_DIM
    ones64 = jnp.ones((tm, HEAD_DIM), F32)

    naq_o[...] = (_group_rms(u[:, _C_NAQ:_C_NAQ + 256], bd, inv64) * naqn_ref[...]).astype(BF16)
    nak_o[...] = (_group_rms(u[:, _C_NAK:_C_NAK + 256], bd, inv64) * nakn_ref[...]).astype(BF16)
    nav_o[...] = u[:, _C_NAV:_C_NAV + 256].astype(BF16)

    cq, slq, shq = cq_ref[...], slq_ref[...], shq_ref[...]
    gq = _rope(_group_rms(u[:, _C_GQ:_C_GQ + 256], bd, inv64) * gqn_ref[...], cq, slq, shq, 16)
    for hh in range(N_HEADS):
        gq_o[hh] = gq[:, 64 * hh:64 * hh + 64].astype(BF16)
    gk = _rope(_group_rms(u[:, _C_GK:_C_GK + 128], bd[:128, :128], inv64) * gkn_ref[...],
               cq[:, :128], slq[:, :128], shq[:, :128], 16)
    gv = u[:, _C_GV:_C_GV + 128]
    for hh in range(GQA_KV_HEADS):
        gk_o[hh] = gk[:, 64 * hh:64 * hh + 64].astype(BF16)
        gv_o[hh] = jnp.concatenate([gv[:, 64 * hh:64 * hh + 64], ones64], axis=-1).astype(BF16)

    gm = gm_ref[...]
    invn = invn_ref[...]
    cm, slm, shm = cm_ref[...], slm_ref[...], shm_ref[...]
    cqn = _rms(u[:, _C_MCQ:_C_MCQ + 256]) * qag_ref[...]
    mq = jnp.dot(cqn.astype(BF16), wuq_ref[...], preferred_element_type=F32)
    mq = _group_rms(mq, gm, invn) * mqn_ref[...]
    mq = _rope(mq, _tile4(cm), _tile4(slm), _tile4(shm), 8)
    ckvn = (_rms(u[:, _C_MCKV:_C_MCKV + 128]) * kvag_ref[...]).astype(BF16)
    mkn = jnp.dot(ckvn, wukvk_ref[...], preferred_element_type=F32)
    mkn = _group_rms(mkn, gm, invn) * mknn_ref[...]
    mvv = jnp.dot(ckvn, wukvv_ref[...], preferred_element_type=F32)
    kr = _group_rms(u[:, _C_MKR:_C_MKR + 128], gm[:128, :128], invn[:, :128]) * mknr_ref[...]
    kr = _rope(kr, cm, slm, shm, 8)
    for hh in range(N_HEADS):
        mq_o[hh] = mq[:, 128 * hh:128 * hh + 128].astype(BF16)
        mk_o[hh] = (mkn[:, 128 * hh:128 * hh + 128] + kr).astype(BF16)
        mv_o[hh] = jnp.concatenate([mvv[:, 64 * hh:64 * hh + 64], ones64], axis=-1).astype(BF16)

    lx_o[...] = u[:, _C_LX:_C_LX + 256]
    g = u[:, _C_LG:_C_LG + 256]
    lg_o[...] = 0.5 * g * (1.0 + jnp.tanh(0.7978845608028654 * (g + 0.044715 * g * g * g)))


def _premix(x, mod, S, tm, consts, lw):
    T, D = x.shape
    nL = S // tm
    row = lambda w: pl.BlockSpec((tm, w), lambda i: (i, 0))
    hm = lambda hn, w: pl.BlockSpec((hn, tm, w), lambda i: (0, i, 0))
    in_specs = [
        row(D),
        pl.BlockSpec((1, 6, D), lambda i: (jnp.where(i >= nL, 1, 0), 0, 0)),
        _full((1, D)), _full((D, IN_COLS_R)), _full((256, 256)), _full((512, 512)),
        row(256), row(256), row(256), row(128), row(128), row(128),
        _full((1, 256)), _full((1, 256)), _full((1, 256)), _full((1, 128)),
        _full((1, 256)), _full((1, 128)),
        _full((1, 512)), _full((1, 512)), _full((1, 128)), _full((1, 512)),
        _full((256, 512)), _full((128, 512)), _full((128, 256)),
    ]
    out_specs = [row(256), row(256), row(256),
                 hm(4, 64), hm(2, 64), hm(2, 128),
                 hm(4, 128), hm(4, 128), hm(4, 128),
                 row(256), row(256)]
    sds = jax.ShapeDtypeStruct
    out_shape = [sds((T, 256), BF16), sds((T, 256), BF16), sds((T, 256), BF16),
                 sds((4, T, 64), BF16), sds((2, T, 64), BF16), sds((2, T, 128), BF16),
                 sds((4, T, 128), BF16), sds((4, T, 128), BF16), sds((4, T, 128), BF16),
                 sds((T, 256), F32), sds((T, 256), F32)]
    return pl.pallas_call(
        _premix_kernel, grid=(T // tm,), in_specs=in_specs, out_specs=out_specs,
        out_shape=out_shape, compiler_params=_params(1), name="premix",
    )(x, mod, lw["g1"], lw["w_in"], consts["bd64"], consts["gm"],
      consts["cq"], consts["slq"], consts["shq"], consts["cm"], consts["slm"], consts["shm"],
      lw["na_qn"], lw["na_kn"], lw["gqa_qn"], lw["gqa_kn"], lw["qa_g"], lw["kva_g"],
      lw["mla_qn"], lw["mla_knn"], lw["mla_knr"], consts["invn"],
      lw["wuq"], lw["wukv_k"], lw["wukv_v"])


def _flash_kernel(q_ref, k_ref, v_ref, o_ref, m_scr, acc_scr, *, group):
    j = pl.program_id(1)
    n_kv = k_ref.shape[0]
    tq = q_ref.shape[1]

    @pl.when(j == 0)
    def _():
        m_scr[...] = jnp.full(m_scr.shape, NEG, F32)
        acc_scr[...] = jnp.zeros(acc_scr.shape, F32)

    for hk in range(n_kv):
        kk = k_ref[hk]
        vv = v_ref[hk]
        q = q_ref[hk * group:(hk + 1) * group].reshape(group * tq, q_ref.shape[2])
        s = lax.dot_general(q, kk, (((1,), (1,)), ((), ())), preferred_element_type=F32)
        m_old = m_scr[hk]
        m_new = jnp.maximum(m_old, jnp.max(s, axis=-1, keepdims=True))
        alpha = jnp.exp(m_old - m_new)
        p = jnp.exp(s - m_new).astype(BF16)
        acc_scr[hk] = alpha * acc_scr[hk] + jnp.dot(p, vv, preferred_element_type=F32)
        m_scr[hk] = m_new

    @pl.when(j == pl.num_programs(1) - 1)
    def _():
        outs = []
        for hk in range(n_kv):
            a = acc_scr[hk]
            o = a[:, :HEAD_DIM] / a[:, HEAD_DIM:HEAD_DIM + 1]
            for g in range(group):
                outs.append(o[g * tq:(g + 1) * tq])
        o_ref[...] = jnp.concatenate(outs, axis=-1)


def _flash(q, k, v, tq, tk):
    hq, nq, d = q.shape
    hk, nk, _ = k.shape
    group = hq // hk
    return pl.pallas_call(
        functools.partial(_flash_kernel, group=group),
        grid=(nq // tq, nk // tk),
        in_specs=[pl.BlockSpec((hq, tq, d), lambda i, j: (0, i, 0)),
                  pl.BlockSpec((hk, tk, d), lambda i, j: (0, j, 0)),
                  pl.BlockSpec((hk, tk, 128), lambda i, j: (0, j, 0))],
        out_specs=pl.BlockSpec((tq, hq * HEAD_DIM), lambda i, j: (i, 0)),
        out_shape=jax.ShapeDtypeStruct((nq, hq * HEAD_DIM), F32),
        scratch_shapes=[pltpu.VMEM((hk, group * tq, 1), F32),
                        pltpu.VMEM((hk, group * tq, 128), F32)],
        compiler_params=_params(2), name="flash_attn",
    )(q, k, v)


def _na_kernel(q_ref, k_ref, v_ref, tab_ref, o_ref, *, rows, n_ctx):
    i = pl.program_id(0)
    s_lat = rows * GRID_W
    kc = k_ref[pl.ds(s_lat, n_ctx), :]
    vc = v_ref[pl.ds(s_lat, n_ctx), :]
    q = q_ref[...]
    nt = (((1,), (1,)), ((), ()))

    @pl.when(i < rows)
    def _():
        r0 = jnp.clip(i - NA_KH // 2, 0, rows - NA_KH)
        off = i - r0
        start = pl.multiple_of(r0 * GRID_W, GRID_W)
        ks = k_ref[pl.ds(start, NA_KH * GRID_W), :]
        vs = v_ref[pl.ds(start, NA_KH * GRID_W), :]
        outs = []
        for h in range(N_HEADS):
            sl = slice(HEAD_DIM * h, HEAD_DIM * (h + 1))
            qh = q[:, sl]
            s_nb = lax.dot_general(qh, ks[:, sl], nt, preferred_element_type=F32) + tab_ref[h, off]
            s_cx = lax.dot_general(qh, kc[:, sl], nt, preferred_element_type=F32)
            m = jnp.maximum(jnp.max(s_nb, axis=-1, keepdims=True),
                            jnp.max(s_cx, axis=-1, keepdims=True))
            p_nb = jnp.exp(s_nb - m)
            p_cx = jnp.exp(s_cx - m)
            l = jnp.sum(p_nb, axis=-1, keepdims=True) + jnp.sum(p_cx, axis=-1, keepdims=True)
            o = (jnp.dot(p_nb.astype(BF16), vs[:, sl], preferred_element_type=F32)
                 + jnp.dot(p_cx.astype(BF16), vc[:, sl], preferred_element_type=F32))
            outs.append(o / l)
        o_ref[...] = jnp.concatenate(outs, axis=-1)

    @pl.when(i >= rows)
    def _():
        outs = []
        for h in range(N_HEADS):
            sl = slice(HEAD_DIM * h, HEAD_DIM * (h + 1))
            s_cx = lax.dot_general(q[:, sl], kc[:, sl], nt, preferred_element_type=F32)
            m = jnp.max(s_cx, axis=-1, keepdims=True)
            p_cx = jnp.exp(s_cx - m)
            l = jnp.sum(p_cx, axis=-1, keepdims=True)
            o = jnp.dot(p_cx.astype(BF16), vc[:, sl], preferred_element_type=F32)
            outs.append(o / l)
        o_ref[...] = jnp.concatenate(outs, axis=-1)


def _neighbourhood(q, k, v, tab, S):
    T = q.shape[0]
    rows = S // GRID_W
    n_ctx = T - S
    resident = lambda shape: pl.BlockSpec(shape, lambda i: (0,) * len(shape),
                                          pipeline_mode=pl.Buffered(1))
    return pl.pallas_call(
        functools.partial(_na_kernel, rows=rows, n_ctx=n_ctx),
        grid=(T // GRID_W,),
        in_specs=[pl.BlockSpec((GRID_W, 256), lambda i: (i, 0)),
                  resident((T, 256)), resident((T, 256)),
                  resident((N_HEADS, NA_KH, GRID_W, NA_KH * GRID_W))],
        out_specs=pl.BlockSpec((GRID_W, 256), lambda i: (i, 0)),
        out_shape=jax.ShapeDtypeStruct((T, 256), F32),
        compiler_params=_params(1), name="nbr_attn",
    )(q, k, v, tab)


def _lru_kernel(x_ref, xp_ref, xn_ref, cw_ref, cb_ref, wa_ref, ba_ref, wi_ref, bi_ref,
                lam_ref, y_ref, h_scr, *, reverse, n_lat, n_ctx):
    j = pl.program_id(0)
    tc = x_ref.shape[0]
    n_all = n_lat + n_ctx
    if reverse:
        chunk = jnp.where(j < n_ctx, n_all - 1 - j, n_all - 1 - j)
    else:
        chunk = jnp.where(j < n_ctx, n_lat + j, j - n_ctx)
    has_prev = jnp.logical_and(chunk != 0, chunk != n_lat)
    has_next = jnp.logical_and(chunk != n_lat - 1, chunk != n_all - 1)

    @pl.when(j == 0)
    def _():
        h_scr[...] = jnp.zeros(h_scr.shape, F32)

    xp = jnp.where(has_prev, xp_ref[...], 0.0)
    xn = jnp.where(has_next, xn_ref[...], 0.0)
    xe = jnp.concatenate([xp, x_ref[...], xn], axis=0)
    ne = tc + 16
    cw = cw_ref[...]
    xc = (cw[0:1] * pltpu.roll(xe, 1, 0)[8:8 + tc]
          + cw[1:2] * xe[8:8 + tc]
          + cw[2:3] * pltpu.roll(xe, ne - 1, 0)[8:8 + tc]
          + cw[3:4] * pltpu.roll(xe, ne - 2, 0)[8:8 + tc]
          + cb_ref[...])

    r = jax.nn.sigmoid(jnp.dot(xc, wa_ref[0], preferred_element_type=F32, precision=HIGHEST)
                       + ba_ref[0])
    gi = jax.nn.sigmoid(jnp.dot(xc, wi_ref[0], preferred_element_type=F32, precision=HIGHEST)
                        + bi_ref[0])
    z = -lam_ref[0]
    softplus = jnp.maximum(z, 0.0) + jnp.log1p(jnp.exp(-jnp.abs(z)))
    log_a = -LRU_C * r * softplus
    a = jnp.exp(log_a)
    b = jnp.sqrt(-jnp.tanh(log_a) * (a * a + 1.0)) * (gi * xc)

    rowi = lax.broadcasted_iota(jnp.int32, (tc, 1), 0)
    s = 1
    while s < tc:
        if reverse:
            keep = rowi < tc - s
            a_s = jnp.where(keep, pltpu.roll(a, tc - s, 0), 1.0)
            b_s = jnp.where(keep, pltpu.roll(b, tc - s, 0), 0.0)
        else:
            keep = rowi >= s
            a_s = jnp.where(keep, pltpu.roll(a, s, 0), 1.0)
            b_s = jnp.where(keep, pltpu.roll(b, s, 0), 0.0)
        b = a * b_s + b
        a = a * a_s
        s *= 2
    hcur = b + a * h_scr[...]
    y_ref[...] = hcur
    h_scr[...] = hcur[0:1] if reverse else hcur[tc - 1:tc]


def _lru_scan(lx, lw, S, tc, reverse):
    T = lx.shape[0]
    n_lat, n_ctx = S // tc, (T - S) // tc
    n_all = n_lat + n_ctx
    d = 1 if reverse else 0
    t8 = tc // 8

    def chunk_of(j):
        if reverse:
            return n_all - 1 - j
        return jnp.where(j < n_ctx, n_lat + j, j - n_ctx)

    dspec = lambda shape: pl.BlockSpec((1,) + shape, lambda j: (d,) + (0,) * len(shape))
    return pl.pallas_call(
        functools.partial(_lru_kernel, reverse=reverse, n_lat=n_lat, n_ctx=n_ctx),
        grid=(n_all,),
        in_specs=[pl.BlockSpec((tc, 256), lambda j: (chunk_of(j), 0)),
                  pl.BlockSpec((8, 256), lambda j: (jnp.maximum(chunk_of(j) * t8 - 1, 0), 0)),
                  pl.BlockSpec((8, 256),
                               lambda j: (jnp.minimum((chunk_of(j) + 1) * t8, T // 8 - 1), 0)),
                  _full((4, 256)), _full((1, 256)),
                  dspec((256, 256)), dspec((1, 256)), dspec((256, 256)), dspec((1, 256)),
                  dspec((1, 256))],
        out_specs=pl.BlockSpec((tc, 256), lambda j: (chunk_of(j), 0)),
        out_shape=jax.ShapeDtypeStruct((T, 256), F32),
        scratch_shapes=[pltpu.VMEM((1, 256), F32)],
        compiler_params=_params(1), name="rglru_bwd" if reverse else "rglru_fwd",
    )(lx, lx, lx, lw["conv_w"], lw["conv_b"], lw["wa"], lw["ba"], lw["wi"], lw["bi"], lw["lam"])


def _merge_kernel(x_ref, mod_ref, oa_ref, ob_ref, oc_ref, yf_ref, yb_ref, lg_ref,
                  gg_ref, wout_ref, g2_ref, rw_ref, rb_ref, lt_ref,
                  x1_o, tok_o, e_o, gate_o, rank_o, cnt_o, cnt_scr):
    i = pl.program_id(0)
    tm = x_ref.shape[0]

    @pl.when(i == 0)
    def _():
        cnt_scr[...] = jnp.zeros(cnt_scr.shape, F32)

    gg = gg_ref[...]
    od = (yf_ref[...] + yb_ref[...]) * lg_ref[...]
    parts = [oa_ref[...], ob_ref[...], oc_ref[...], od]
    ycat = jnp.concatenate(
        [(_rms(p) * gg[:, GROUP_W * n:GROUP_W * (n + 1)]).astype(BF16) for n, p in enumerate(parts)],
        axis=-1)
    y = jnp.dot(ycat, wout_ref[...], preferred_element_type=F32)
    gt1 = mod_ref[0, 2:3, :]
    sh2 = mod_ref[0, 3:4, :]
    sc2 = mod_ref[0, 4:5, :]
    x1 = x_ref[...] + gt1 * y
    x1_o[...] = x1
    tok = _rms(x1) * g2_ref[...] * (1.0 + sc2) + sh2
    tok_o[...] = tok

    logits = jnp.dot(tok, rw_ref[...], preferred_element_type=F32, precision=HIGHEST) + rb_ref[...]
    lane = lax.broadcasted_iota(jnp.int32, (tm, LANES), 1)
    e_acc = jnp.zeros((tm, LANES), jnp.int32)
    v_acc = jnp.zeros((tm, LANES), F32)
    onehots = []
    v0 = None
    work = logits
    for k in range(TOP_K):
        vk = jnp.max(work, axis=-1, keepdims=True)
        ek = jnp.min(jnp.where(work == vk, lane, LANES), axis=-1, keepdims=True)
        sel = lane == ek
        onehots.append(sel)
        work = jnp.where(sel, NEG * 2.0, work)
        if k == 0:
            v0 = vk
        e_acc = jnp.where(lane == k, ek, e_acc)
        v_acc = jnp.where(lane == k, jnp.exp(vk - v0), v_acc)
    gate_o[...] = v_acc / jnp.sum(v_acc, axis=-1, keepdims=True)
    e_o[...] = e_acc

    oh = [jnp.where(o, 1.0, 0.0) for o in onehots]
    oh_all = oh[0] + oh[1] + oh[2] + oh[3]
    excl = cnt_scr[...] + jnp.dot(lt_ref[...], oh_all.astype(BF16), preferred_element_type=F32)
    rank = jnp.zeros((tm, LANES), F32)
    for k in range(TOP_K):
        rk = jnp.sum(oh[k] * excl, axis=-1, keepdims=True)
        rank = jnp.where(lane == k, rk, rank)
    rank_o[...] = rank.astype(jnp.int32)
    cnt = cnt_scr[...] + jnp.sum(oh_all, axis=0, keepdims=True)
    cnt_scr[...] = cnt
    cnt_o[...] = jnp.broadcast_to(cnt, cnt_o.shape)


def _merge(x, mod, oa, ob, oc, yf, yb, lg, S, tm, consts, lw):
    T, D = x.shape
    nL = S // tm
    row = lambda w: pl.BlockSpec((tm, w), lambda i: (i, 0))
    sds = jax.ShapeDtypeStruct
    return pl.pallas_call(
        _merge_kernel, grid=(T // tm,),
        in_specs=[row(D), pl.BlockSpec((1, 6, D), lambda i: (jnp.where(i >= nL, 1, 0), 0, 0)),
                  row(256), row(256), row(256), row(256), row(256), row(256),
                  _full((1, D)), _full((D, D)), _full((1, D)), _full((D, LANES)),
                  _full((1, LANES)), _full((tm, tm))],
        out_specs=[row(D), row(D), row(LANES), row(LANES), row(LANES), _full((8, LANES))],
        out_shape=[sds((T, D), F32), sds((T, D), F32), sds((T, LANES), jnp.int32),
                   sds((T, LANES), F32), sds((T, LANES), jnp.int32), sds((8, LANES), F32)],
        scratch_shapes=[pltpu.VMEM((1, LANES), F32)],
        compiler_params=_params(1), name="merge_router",
    )(x, mod, oa, ob, oc, yf, yb, lg, lw["grp_g"], lw["w_out"], lw["g2"], lw["router_w"],
      lw["router_b"], consts["lt"])


def _dispatch_kernel(pos_ref, tok_ref, xin_in_ref, xin_ref, sem, *, tt):
    del xin_in_ref
    base = pl.program_id(0) * tt
    n = tt * TOP_K

    def copy(j):
        return pltpu.make_async_copy(tok_ref.at[pl.ds(base + j // TOP_K, 1)],
                                     xin_ref.at[pl.ds(pos_ref[0, 0, j], 1)], sem)

    def issue(j, c):
        copy(j).start()
        return c

    def drain(j, c):
        copy(j).wait()
        return c

    lax.fori_loop(0, n, issue, 0)
    lax.fori_loop(0, n, drain, 0)


def _dispatch(tok, pos, n_rows, tt):
    T, D = tok.shape
    pos3 = pos.reshape(T // tt, 1, tt * TOP_K)
    zeros = jnp.zeros((n_rows, D), F32)
    return pl.pallas_call(
        functools.partial(_dispatch_kernel, tt=tt),
        grid=(T // tt,),
        in_specs=[pl.BlockSpec((1, 1, tt * TOP_K), lambda i: (i, 0, 0),
                               memory_space=pltpu.SMEM),
                  pl.BlockSpec(memory_space=pl.ANY),
                  pl.BlockSpec(memory_space=pl.ANY)],
        out_specs=pl.BlockSpec(memory_space=pl.ANY),
        out_shape=jax.ShapeDtypeStruct((n_rows, D), F32),
        scratch_shapes=[pltpu.SemaphoreType.DMA(())],
        input_output_aliases={2: 0},
        compiler_params=_params(1), name="moe_dispatch",
    )(pos3, tok, zeros)


def _expert_kernel(be_ref, nu_ref, x_ref, wgu_ref, bgu_ref, wd_ref, bd_ref, y_ref):
    i = pl.program_id(0)
    d_e = wd_ref.shape[1]

    @pl.when(i < nu_ref[0])
    def _():
        gu = jnp.dot(x_ref[...].astype(BF16), wgu_ref[0], preferred_element_type=F32) + bgu_ref[0]
        x_glu = jnp.minimum(gu[:, :d_e], SWIGLU_LIMIT)
        x_lin = jnp.clip(gu[:, d_e:], -SWIGLU_LIMIT, SWIGLU_LIMIT)
        act = x_glu * jax.nn.sigmoid(SWIGLU_ALPHA * x_glu) * (x_lin + 1.0)
        y_ref[...] = jnp.dot(act.astype(BF16), wd_ref[0], preferred_element_type=F32) + bd_ref[0]

    @pl.when(i >= nu_ref[0])
    def _():
        y_ref[...] = jnp.zeros(y_ref.shape, F32)


def _experts(xin, blk_expert, n_used, w_gu, b_gu, w_down, b_down, bm):
    P, D = xin.shape
    E, _, d2 = w_gu.shape
    d_e = d2 // 2
    blk = lambda i, be, nu: jnp.minimum(i, nu[0] - 1)
    grid_spec = pltpu.PrefetchScalarGridSpec(
        num_scalar_prefetch=2, grid=(P // bm,),
        in_specs=[pl.BlockSpec((bm, D), lambda i, be, nu: (blk(i, be, nu), 0)),
                  pl.BlockSpec((1, D, d2), lambda i, be, nu: (be[blk(i, be, nu)], 0, 0)),
                  pl.BlockSpec((1, 1, d2), lambda i, be, nu: (be[blk(i, be, nu)], 0, 0)),
                  pl.BlockSpec((1, d_e, D), lambda i, be, nu: (be[blk(i, be, nu)], 0, 0)),
                  pl.BlockSpec((1, 1, D), lambda i, be, nu: (be[blk(i, be, nu)], 0, 0))],
        out_specs=pl.BlockSpec((bm, D), lambda i, be, nu: (i, 0)))
    return pl.pallas_call(
        _expert_kernel, grid_spec=grid_spec,
        out_shape=jax.ShapeDtypeStruct((P, D), F32),
        compiler_params=_params(1), name="moe_experts",
    )(blk_expert, n_used, xin, w_gu, b_gu.reshape(E, 1, d2), w_down, b_down.reshape(E, 1, D))


def _combine_kernel(pos_ref, y_ref, gate_ref, x_ref, mod_ref, o_ref, buf, sem, *, tt):
    n = tt * TOP_K

    def copy(j):
        return pltpu.make_async_copy(y_ref.at[pl.ds(pos_ref[0, 0, j], 1)],
                                     buf.at[j % TOP_K, pl.ds(j // TOP_K, 1)], sem)

    def issue(j, c):
        copy(j).start()
        return c

    def drain(j, c):
        copy(j).wait()
        return c

    lax.fori_loop(0, n, issue, 0)
    lax.fori_loop(0, n, drain, 0)
    gate = gate_ref[...]
    f = gate[:, 0:1] * buf[0]
    for k in range(1, TOP_K):
        f = f + gate[:, k:k + 1] * buf[k]
    o_ref[...] = x_ref[...] + mod_ref[0, 5:6, :] * f


def _combine(y, pos, gate, x1, mod, S, tt):
    T, D = x1.shape
    nL = S // tt
    pos3 = pos.reshape(T // tt, 1, tt * TOP_K)
    return pl.pallas_call(
        functools.partial(_combine_kernel, tt=tt),
        grid=(T // tt,),
        in_specs=[pl.BlockSpec((1, 1, tt * TOP_K), lambda i: (i, 0, 0),
                               memory_space=pltpu.SMEM),
                  pl.BlockSpec(memory_space=pl.ANY),
                  pl.BlockSpec((tt, LANES), lambda i: (i, 0)),
                  pl.BlockSpec((tt, D), lambda i: (i, 0)),
                  pl.BlockSpec((1, 6, D), lambda i: (jnp.where(i >= nL, 1, 0), 0, 0))],
        out_specs=pl.BlockSpec((tt, D), lambda i: (i, 0)),
        out_shape=jax.ShapeDtypeStruct((T, D), F32),
        scratch_shapes=[pltpu.VMEM((TOP_K, tt, D), F32), pltpu.SemaphoreType.DMA(())],
        compiler_params=_params(1), name="moe_combine",
    )(pos3, y, gate, x1, mod)


def _rope_tables(S, T, rot_dim, width, lane0):
    t = jnp.arange(S)
    rowf = (t // GRID_W).astype(F32)
    colf = (t % GRID_W).astype(F32)
    ax = rot_dim // 2
    inv = ROPE_THETA ** (-jnp.arange(0, ax, 2, dtype=F32) / ax)
    ang = jnp.concatenate([rowf[:, None] * inv, colf[:, None] * inv], axis=-1)
    cos, sin = jnp.cos(ang), jnp.sin(ang)
    h = rot_dim // 4
    zero = jnp.zeros((S, h), F32)
    cos_f = jnp.concatenate([cos[:, :h], cos[:, :h], cos[:, h:], cos[:, h:]], axis=-1)
    s_lo = jnp.concatenate([zero, sin[:, :h], zero, sin[:, h:]], axis=-1)
    s_hi = jnp.concatenate([-sin[:, :h], zero, -sin[:, h:], zero], axis=-1)

    def place(tab, fill):
        out = jnp.full((T, width), fill, F32)
        return out.at[:S, lane0:lane0 + rot_dim].set(tab)

    return place(cos_f, 1.0), place(s_lo, 0.0), place(s_hi, 0.0)


def _constants(S, T, tm):
    cq, slq, shq = _rope_tables(S, T, HEAD_DIM, HEAD_DIM, 0)
    cq, slq, shq = (jnp.tile(a, (1, N_HEADS)) for a in (cq, slq, shq))
    cm, slm, shm = _rope_tables(S, T, MLA_ROPE, MLA_PAD, MLA_NOPE)
    bd64 = np.kron(np.eye(4, dtype=np.float32), np.ones((64, 64), np.float32))
    sizes = [MLA_NOPE, MLA_ROPE, MLA_PAD - MLA_NOPE - MLA_ROPE] * N_HEADS
    gid = np.repeat(np.arange(len(sizes)), sizes)
    gm = (gid[:, None] == gid[None, :]).astype(np.float32)
    invn = (1.0 / np.repeat(np.asarray(sizes, np.float32), sizes))[None, :]
    lt = np.tril(np.ones((tm, tm), np.float32), -1)
    return dict(cq=cq, slq=slq, shq=shq, cm=cm, slm=slm, shm=shm,
                bd64=jnp.asarray(bd64, BF16), gm=jnp.asarray(gm, BF16),
                invn=jnp.asarray(invn), lt=jnp.asarray(lt, BF16))


def _na_bias_index():
    off = np.arange(NA_KH)[:, None, None, None]
    qc = np.arange(GRID_W)[None, :, None, None]
    j = np.arange(NA_KH)[None, None, :, None]
    kc = np.arange(GRID_W)[None, None, None, :]
    a = np.broadcast_to(j - off + NA_KH - 1, (NA_KH, GRID_W, NA_KH, GRID_W))
    c0 = np.clip(qc - NA_KW // 2, 0, GRID_W - NA_KW)
    valid = np.broadcast_to((kc >= c0) & (kc < c0 + NA_KW), a.shape)
    b = np.clip(np.broadcast_to(kc - qc + NA_KW - 1, a.shape), 0, 2 * NA_KW - 2)
    shape = (NA_KH, GRID_W, NA_KH * GRID_W)
    return a.reshape(shape), b.reshape(shape), valid.reshape(shape)


def _block_diag(w):
    n, bw, _ = w.shape
    eye = jnp.eye(n, dtype=w.dtype)
    return (eye[:, None, :, None] * w[:, :, None, :]).reshape(n * bw, n * bw)


def _layer_weights(l, p):
    D = p["w_in"].shape[1]
    w_in = p["w_in"][l]
    offs = np.cumsum([0, 256, 256, 256, 256, 128, 128, 256, 128, 32, 256, 256])
    seg = lambda n: w_in[:, offs[n]:offs[n + 1]]
    z = lambda n: jnp.zeros((D, n), F32)
    w_in_r = jnp.concatenate(
        [seg(0), seg(1), seg(2), seg(3), seg(4), seg(5), seg(6), seg(7), seg(9), seg(10),
         z(MLA_NOPE), seg(8), z(MLA_PAD - MLA_NOPE - MLA_ROPE)], axis=-1).astype(BF16)
    sc = HEAD_DIM ** -0.5
    sc_m = (MLA_NOPE + MLA_ROPE) ** -0.5
    t4 = lambda g: jnp.tile(g, N_HEADS)[None, :]
    wuq = p["mla_wuq"][l].reshape(-1, N_HEADS, MLA_NOPE + MLA_ROPE)
    wuq = jnp.pad(wuq, ((0, 0), (0, 0), (0, MLA_PAD - MLA_NOPE - MLA_ROPE)))
    wukv = p["mla_wukv"][l].reshape(-1, N_HEADS, MLA_NOPE + HEAD_DIM)
    wukv_k = jnp.pad(wukv[:, :, :MLA_NOPE], ((0, 0), (0, 0), (0, MLA_PAD - MLA_NOPE)))
    qn, kn = p["mla_qn"][l], p["mla_kn"][l]
    padq = jnp.pad(qn * sc_m, (0, MLA_PAD - MLA_NOPE - MLA_ROPE))
    padkn = jnp.pad(kn[:MLA_NOPE], (0, MLA_PAD - MLA_NOPE))
    padkr = jnp.pad(kn[MLA_NOPE:], (MLA_NOPE, MLA_PAD - MLA_NOPE - MLA_ROPE))
    a_idx, b_idx, valid = _na_bias_index()
    tab = jnp.where(valid[None], p["na_rpb"][l][:, a_idx, b_idx], NEG)
    rw = jnp.pad(p["router_w"][l], ((0, 0), (0, LANES - N_EXPERTS)))
    rb = jnp.pad(p["router_b"][l], (0, LANES - N_EXPERTS), constant_values=NEG)[None, :]
    return dict(
        g1=p["norm1_g"][l][None, :], g2=p["norm2_g"][l][None, :], w_in=w_in_r,
        na_qn=t4(p["na_qn"][l] * sc), na_kn=t4(p["na_kn"][l]),
        gqa_qn=t4(p["gqa_qn"][l] * sc), gqa_kn=jnp.tile(p["gqa_kn"][l], GQA_KV_HEADS)[None, :],
        qa_g=p["mla_qa_g"][l][None, :], kva_g=p["mla_kva_g"][l][None, :],
        mla_qn=t4(padq), mla_knn=t4(padkn), mla_knr=padkr[None, :],
        wuq=wuq.reshape(-1, N_HEADS * MLA_PAD).astype(BF16),
        wukv_k=wukv_k.reshape(-1, N_HEADS * MLA_PAD).astype(BF16),
        wukv_v=wukv[:, :, MLA_NOPE:].reshape(-1, N_HEADS * HEAD_DIM).astype(BF16),
        na_tab=tab,
        conv_w=p["lru_conv_w"][l][:, 0, :], conv_b=p["lru_conv_b"][l][None, :],
        wa=jnp.stack([_block_diag(p["lru_wa"][l][d]) for d in range(2)]),
        wi=jnp.stack([_block_diag(p["lru_wi"][l][d]) for d in range(2)]),
        ba=p["lru_ba"][l][:, None, :], bi=p["lru_bi"][l][:, None, :],
        lam=p["lru_lam"][l][:, None, :],
        grp_g=p["grp_g"][l][None, :], w_out=p["w_out"][l].astype(BF16),
        router_w=rw, router_b=rb,
        w_gu=p["exp_w_gu"][l].astype(BF16), b_gu=p["exp_b_gu"][l],
        w_down=p["exp_w_down"][l].astype(BF16), b_down=p["exp_b_down"][l],
    )


def _tiles(S, C):
    tm = min(256, C)
    tq = min(512, S)
    T = S + C
    tk = next(t for t in (1280, 640, 256, 128) if T % t == 0)
    return dict(tm=tm, tq=tq, tk=tk, tc=min(256, C), bm=256, tt=min(128, C))


def kernel(x, c, ctx, c_ctx, ada_w, ada_b, norm1_g, norm2_g, w_in, na_qn, na_kn, na_rpb, gqa_qn, gqa_kn, mla_qa_g, mla_kva_g, mla_wuq, mla_wukv, mla_qn, mla_kn, lru_conv_w, lru_conv_b, lru_wa, lru_ba, lru_wi, lru_bi, lru_lam, grp_g, w_out, router_w, router_b, exp_w_gu, exp_b_gu, exp_w_down, exp_b_down):
    p = dict(norm1_g=norm1_g, norm2_g=norm2_g, w_in=w_in, na_qn=na_qn, na_kn=na_kn, na_rpb=na_rpb,
             gqa_qn=gqa_qn, gqa_kn=gqa_kn, mla_qa_g=mla_qa_g, mla_kva_g=mla_kva_g,
             mla_wuq=mla_wuq, mla_wukv=mla_wukv, mla_qn=mla_qn, mla_kn=mla_kn,
             lru_conv_w=lru_conv_w, lru_conv_b=lru_conv_b, lru_wa=lru_wa, lru_ba=lru_ba,
             lru_wi=lru_wi, lru_bi=lru_bi, lru_lam=lru_lam, grp_g=grp_g, w_out=w_out,
             router_w=router_w, router_b=router_b, exp_w_gu=exp_w_gu, exp_b_gu=exp_b_gu,
             exp_w_down=exp_w_down, exp_b_down=exp_b_down)
    B, S, D = x.shape
    assert B == 1 and S % GRID_W == 0
    C = ctx.shape[1]
    T = S + C
    L = ada_w.shape[0]
    ts = _tiles(S, C)
    tm, bm = ts["tm"], ts["bm"]
    consts = _constants(S, T, tm)

    c8 = jnp.zeros((8, D), F32).at[0].set(c[0]).at[1].set(c_ctx)
    mods = _modulation(c8, ada_w, ada_b)[:, :2].reshape(L, 2, 6, D)

    n_rows = (T * TOP_K + N_EXPERTS * (bm - 1)) // bm * bm
    xs = jnp.concatenate([x[0], ctx[0]], axis=0)
    for l in range(L):
        lw = _layer_weights(l, p)
        mod = mods[l]
        (naq, nak, nav, gq, gk, gv, mq, mk, mv, lx, lg) = _premix(xs, mod, S, tm, consts, lw)
        oa = _neighbourhood(naq, nak, nav, lw["na_tab"], S)
        ob = jnp.concatenate([_flash(gq[:, :S], gk, gv, ts["tq"], ts["tk"]),
                              _flash(gq[:, S:], gk[:, S:], gv[:, S:], C, C)], axis=0)
        oc = jnp.concatenate([_flash(mq[:, :S], mk, mv, ts["tq"], ts["tk"]),
                              _flash(mq[:, S:], mk[:, S:], mv[:, S:], C, C)], axis=0)
        yf = _lru_scan(lx, lw, S, ts["tc"], False)
        yb = _lru_scan(lx, lw, S, ts["tc"], True)
        x1, tok, e_idx, gate, rank, cnt = _merge(xs, mod, oa, ob, oc, yf, yb, lg, S, tm, consts, lw)

        counts = cnt[0, :N_EXPERTS].astype(jnp.int32)
        padded = (counts + bm - 1) // bm * bm
        pad_end = jnp.cumsum(padded)
        pad_start = pad_end - padded
        e4 = e_idx[:, :TOP_K]
        pos = (pad_start[e4] + rank[:, :TOP_K]).astype(jnp.int32)
        n_blocks = n_rows // bm
        blk_expert = jnp.minimum(
            jnp.searchsorted(pad_end, jnp.arange(n_blocks, dtype=jnp.int32) * bm, side="right"),
            N_EXPERTS - 1).astype(jnp.int32)
        n_used = (pad_end[-1:] // bm).astype(jnp.int32)

        xin = _dispatch(tok, pos, n_rows, ts["tt"])
        y = _experts(xin, blk_expert, n_used, lw["w_gu"], lw["b_gu"], lw["w_down"], lw["b_down"], bm)
        xs = _combine(y, pos, gate, x1, mod, S, ts["tt"])
    return xs[:S][None]
```

```python
import functools

import numpy as np
import jax
import jax.numpy as jnp
from jax import lax
from jax.experimental import pallas as pl
from jax.experimental.pallas import tpu as pltpu

F32 = jnp.float32
BF16 = jnp.bfloat16
HIGHEST = lax.Precision.HIGHEST

GRID_W = 64
HEAD_DIM = 64
N_HEADS = 4
GROUP_W = 256
GQA_KV_HEADS = 2
NA_KH = 8
NA_KW = 16
MLA_NOPE = 64
MLA_ROPE = 32
MLA_PAD = 128
LRU_C = 8.0
N_EXPERTS = 32
TOP_K = 4
SWIGLU_LIMIT = 7.0
SWIGLU_ALPHA = 1.702
ROPE_THETA = 10000.0
EPS = 1e-6
NEG = -1e30
LANES = 128
FLASH_ROW_BLOCK = 256
LOG2E = 1.4426950408889634
RUN_ALIGN = 8
VMEM_LIMIT = 56 * 1024 * 1024

_C_NAQ, _C_NAK, _C_NAV = 0, 256, 512
_C_GQ, _C_GK, _C_GV = 768, 1024, 1152
_C_MCQ, _C_MCKV = 1280, 1536
_C_LX, _C_LG = 1664, 1920
_C_MKR = 2176
IN_COLS_R = 2304


def _params(n_axes, vmem=VMEM_LIMIT):
    return pltpu.CompilerParams(dimension_semantics=("arbitrary",) * n_axes,
                                vmem_limit_bytes=vmem)


def _full(shape):
    n = len(shape)
    return pl.BlockSpec(shape, lambda *_: (0,) * n)


def _mod_kernel(c_ref, w_ref, b_ref, o_ref):
    cc = c_ref[...]
    s = cc * jax.nn.sigmoid(cc)
    o_ref[0] = jnp.dot(s, w_ref[0], preferred_element_type=F32, precision=HIGHEST) + b_ref[0]


def _modulation(c8, ada_w, ada_b):
    L, D, D6 = ada_w.shape
    tn = D6 // 6
    return pl.pallas_call(
        _mod_kernel,
        grid=(L, D6 // tn),
        in_specs=[pl.BlockSpec((8, D), lambda l, j: (0, 0)),
                  pl.BlockSpec((1, D, tn), lambda l, j: (l, 0, j)),
                  pl.BlockSpec((1, 1, tn), lambda l, j: (l, 0, j))],
        out_specs=pl.BlockSpec((1, 8, tn), lambda l, j: (l, 0, j)),
        out_shape=jax.ShapeDtypeStruct((L, 8, D6), F32),
        compiler_params=_params(2),
        name="adaln_mod",
    )(c8, ada_w, ada_b.reshape(L, 1, D6))


def _rms(z):
    return z * lax.rsqrt(jnp.mean(z * z, axis=-1, keepdims=True) + EPS)


def _group_rms(z, gmat, inv_n):
    zz = z * z
    hi = zz.astype(BF16)
    lo = (zz - hi.astype(F32)).astype(BF16)
    ss = (jnp.dot(hi, gmat, preferred_element_type=F32)
          + jnp.dot(lo, gmat, preferred_element_type=F32))
    return z * lax.rsqrt(ss * inv_n + EPS)


def _rope(z, cos, sin_lo, sin_hi, half):
    w = z.shape[-1]
    return z * cos + pltpu.roll(z, half, 1) * sin_lo + pltpu.roll(z, w - half, 1) * sin_hi


def _tile4(t):
    return jnp.concatenate([t, t, t, t], axis=-1)


def _premix_kernel(x_ref, mod_ref, g1_ref, win_ref, bd_ref, gm_ref,
                   cq_ref, slq_ref, shq_ref, cm_ref, slm_ref, shm_ref,
                   naqn_ref, nakn_ref, gqn_ref, gkn_ref, qag_ref, kvag_ref,
                   mqn_ref, mknn_ref, mknr_ref, invn_ref, wuq_ref, wukvk_ref, wukvv_ref,
                   naq_o, nak_o, nav_o, gq_o, gk_o, gv_o, mq_o, mk_o, mv_o, lx_o, lg_o):
    tm = x_ref.shape[0]
    xt = x_ref[...]
    sh1 = mod_ref[0, 0:1, :]
    sc1 = mod_ref[0, 1:2, :]
    h = _rms(xt) * g1_ref[...] * (1.0 + sc1) + sh1
    u = jnp.dot(h.astype(BF16), win_ref[...], preferred_element_type=F32)

    bd = bd_ref[...]
    inv64 = 1.0 / HEAD_DIM
    ones64 = jnp.ones((tm, HEAD_DIM), F32)

    naq_o[...] = (_group_rms(u[:, _C_NAQ:_C_NAQ + 256], bd, inv64) * naqn_ref[...]).astype(BF16)
    nak_o[...] = (_group_rms(u[:, _C_NAK:_C_NAK + 256], bd, inv64) * nakn_ref[...]).astype(BF16)
    nav_o[...] = u[:, _C_NAV:_C_NAV + 256].astype(BF16)

    cq, slq, shq = cq_ref[...], slq_ref[...], shq_ref[...]
    gq = _rope(_group_rms(u[:, _C_GQ:_C_GQ + 256], bd, inv64) * gqn_ref[...], cq, slq, shq, 16)
    for hh in range(N_HEADS):
        gq_o[hh] = gq[:, 64 * hh:64 * hh + 64].astype(BF16)
    gk = _rope(_group_rms(u[:, _C_GK:_C_GK + 128], bd[:128, :128], inv64) * gkn_ref[...],
               cq[:, :128], slq[:, :128], shq[:, :128], 16)
    gv = u[:, _C_GV:_C_GV + 128]
    for hh in range(GQA_KV_HEADS):
        gk_o[hh] = gk[:, 64 * hh:64 * hh + 64].astype(BF16)
        gv_o[hh] = jnp.concatenate([gv[:, 64 * hh:64 * hh + 64], ones64], axis=-1).astype(BF16)

    gm = gm_ref[...]
    invn = invn_ref[...]
    cm, slm, shm = cm_ref[...], slm_ref[...], shm_ref[...]
    cqn = _rms(u[:, _C_MCQ:_C_MCQ + 256]) * qag_ref[...]
    mq = jnp.dot(cqn.astype(BF16), wuq_ref[...], preferred_element_type=F32)
    mq = _group_rms(mq, gm, invn) * mqn_ref[...]
    mq = _rope(mq, _tile4(cm), _tile4(slm), _tile4(shm), 8)
    ckvn = (_rms(u[:, _C_MCKV:_C_MCKV + 128]) * kvag_ref[...]).astype(BF16)
    mkn = jnp.dot(ckvn, wukvk_ref[...], preferred_element_type=F32)
    mkn = _group_rms(mkn, gm, invn) * mknn_ref[...]
    mvv = jnp.dot(ckvn, wukvv_ref[...], preferred_element_type=F32)
    kr = _group_rms(u[:, _C_MKR:_C_MKR + 128], gm[:128, :128], invn[:, :128]) * mknr_ref[...]
    kr = _rope(kr, cm, slm, shm, 8)
    for hh in range(N_HEADS):
        mq_o[hh] = mq[:, 128 * hh:128 * hh + 128].astype(BF16)
        mk_o[hh] = (mkn[:, 128 * hh:128 * hh + 128] + kr).astype(BF16)
        mv_o[hh] = jnp.concatenate([mvv[:, 64 * hh:64 * hh + 64], ones64], axis=-1).astype(BF16)

    lx_o[...] = u[:, _C_LX:_C_LX + 256]
    g = u[:, _C_LG:_C_LG + 256]
    lg_o[...] = 0.5 * g * (1.0 + jnp.tanh(0.7978845608028654 * (g + 0.044715 * g * g * g)))


def _premix(x, mod, S, tm, consts, lw):
    T, D = x.shape
    nL = S // tm
    row = lambda w: pl.BlockSpec((tm, w), lambda i: (i, 0))
    hm = lambda hn, w: pl.BlockSpec((hn, tm, w), lambda i: (0, i, 0))
    in_specs = [
        row(D),
        pl.BlockSpec((1, 6, D), lambda i: (jnp.where(i >= nL, 1, 0), 0, 0)),
        _full((1, D)), _full((D, IN_COLS_R)), _full((256, 256)), _full((512, 512)),
        row(256), row(256), row(256), row(128), row(128), row(128),
        _full((1, 256)), _full((1, 256)), _full((1, 256)), _full((1, 128)),
        _full((1, 256)), _full((1, 128)),
        _full((1, 512)), _full((1, 512)), _full((1, 128)), _full((1, 512)),
        _full((256, 512)), _full((128, 512)), _full((128, 256)),
    ]
    out_specs = [row(256), row(256), row(256),
                 hm(4, 64), hm(2, 64), hm(2, 128),
                 hm(4, 128), hm(4, 128), hm(4, 128),
                 row(256), row(256)]
    sds = jax.ShapeDtypeStruct
    out_shape = [sds((T, 256), BF16), sds((T, 256), BF16), sds((T, 256), BF16),
                 sds((4, T, 64), BF16), sds((2, T, 64), BF16), sds((2, T, 128), BF16),
                 sds((4, T, 128), BF16), sds((4, T, 128), BF16), sds((4, T, 128), BF16),
                 sds((T, 256), F32), sds((T, 256), F32)]
    return pl.pallas_call(
        _premix_kernel, grid=(T // tm,), in_specs=in_specs, out_specs=out_specs,
        out_shape=out_shape, compiler_params=_params(1), name="premix",
    )(x, mod, lw["g1"], lw["w_in"], consts["bd64"], consts["gm"],
      consts["cq"], consts["slq"], consts["shq"], consts["cm"], consts["slm"], consts["shm"],
      lw["na_qn"], lw["na_kn"], lw["gqa_qn"], lw["gqa_kn"], lw["qa_g"], lw["kva_g"],
      lw["mla_qn"], lw["mla_knn"], lw["mla_knr"], consts["invn"],
      lw["wuq"], lw["wukv_k"], lw["wukv_v"])


def _flash_kernel(q_ref, k_ref, v_ref, o_ref, m_scr, acc_scr, *, group, rb):
    j = pl.program_id(1)
    n_kv = k_ref.shape[0]
    tq = q_ref.shape[1]

    @pl.when(j == 0)
    def _():
        m_scr[...] = jnp.full(m_scr.shape, NEG, F32)
        acc_scr[...] = jnp.zeros(acc_scr.shape, F32)

    chains = [(hq, slice(r0, r0 + rb)) for hq in range(n_kv * group) for r0 in range(0, tq, rb)]

    def scores(chain):
        hq, rows = chain
        return lax.dot_general(q_ref[hq, rows, :], k_ref[hq // group], (((1,), (1,)), ((), ())),
                               preferred_element_type=F32)

    s = scores(chains[0])
    for n, (hq, rows) in enumerate(chains):
        s_next = scores(chains[n + 1]) if n + 1 < len(chains) else None
        m_old = m_scr[hq, rows, :]
        m_new = jnp.maximum(m_old, jnp.max(s, axis=-1, keepdims=True))
        alpha = jnp.exp2(m_old - m_new)
        p = jnp.exp2(s - m_new).astype(BF16)
        acc_scr[hq, rows, :] = (alpha * acc_scr[hq, rows, :]
                                + jnp.dot(p, v_ref[hq // group], preferred_element_type=F32))
        m_scr[hq, rows, :] = m_new
        s = s_next

    @pl.when(j == pl.num_programs(1) - 1)
    def _():
        outs = []
        for hq in range(n_kv * group):
            a = acc_scr[hq]
            outs.append(a[:, :HEAD_DIM] / a[:, HEAD_DIM:HEAD_DIM + 1])
        o_ref[...] = jnp.concatenate(outs, axis=-1)


def _flash(q, k, v, tq, tk, n_q, q_blk0, n_kv, kv_blk0):
    hq, _, d = q.shape
    hk = k.shape[0]
    group = hq // hk
    return pl.pallas_call(
        functools.partial(_flash_kernel, group=group, rb=min(FLASH_ROW_BLOCK, tq)),
        grid=(n_q, n_kv),
        in_specs=[pl.BlockSpec((hq, tq, d), lambda i, j: (0, q_blk0 + i, 0)),
                  pl.BlockSpec((hk, tk, d), lambda i, j: (0, kv_blk0 + j, 0)),
                  pl.BlockSpec((hk, tk, 128), lambda i, j: (0, kv_blk0 + j, 0))],
        out_specs=pl.BlockSpec((tq, hq * HEAD_DIM), lambda i, j: (i, 0)),
        out_shape=jax.ShapeDtypeStruct((n_q * tq, hq * HEAD_DIM), F32),
        scratch_shapes=[pltpu.VMEM((hq, tq, 1), F32), pltpu.VMEM((hq, tq, 128), F32)],
        compiler_params=_params(2), name="flash_attn",
    )(q, k, v)


def _na_kernel(q_ref, k_ref, v_ref, tab_ref, o_ref, *, rows, n_ctx):
    i = pl.program_id(0)
    s_lat = rows * GRID_W
    kc = k_ref[pl.ds(s_lat, n_ctx), :]
    vc = v_ref[pl.ds(s_lat, n_ctx), :]
    q = q_ref[...]
    nt = (((1,), (1,)), ((), ()))

    @pl.when(i < rows)
    def _():
        r0 = jnp.clip(i - NA_KH // 2, 0, rows - NA_KH)
        off = i - r0
        start = pl.multiple_of(r0 * GRID_W, GRID_W)
        ks = k_ref[pl.ds(start, NA_KH * GRID_W), :]
        vs = v_ref[pl.ds(start, NA_KH * GRID_W), :]
        outs = []
        for h in range(N_HEADS):
            sl = slice(HEAD_DIM * h, HEAD_DIM * (h + 1))
            qh = q[:, sl]
            s_nb = lax.dot_general(qh, ks[:, sl], nt, preferred_element_type=F32) + tab_ref[h, off]
            s_cx = lax.dot_general(qh, kc[:, sl], nt, preferred_element_type=F32)
            m = jnp.maximum(jnp.max(s_nb, axis=-1, keepdims=True),
                            jnp.max(s_cx, axis=-1, keepdims=True))
            p_nb = jnp.exp(s_nb - m)
            p_cx = jnp.exp(s_cx - m)
            l = jnp.sum(p_nb, axis=-1, keepdims=True) + jnp.sum(p_cx, axis=-1, keepdims=True)
            o = (jnp.dot(p_nb.astype(BF16), vs[:, sl], preferred_element_type=F32)
                 + jnp.dot(p_cx.astype(BF16), vc[:, sl], preferred_element_type=F32))
            outs.append(o / l)
        o_ref[...] = jnp.concatenate(outs, axis=-1)

    @pl.when(i >= rows)
    def _():
        outs = []
        for h in range(N_HEADS):
            sl = slice(HEAD_DIM * h, HEAD_DIM * (h + 1))
            s_cx = lax.dot_general(q[:, sl], kc[:, sl], nt, preferred_element_type=F32)
            m = jnp.max(s_cx, axis=-1, keepdims=True)
            p_cx = jnp.exp(s_cx - m)
            l = jnp.sum(p_cx, axis=-1, keepdims=True)
            o = jnp.dot(p_cx.astype(BF16), vc[:, sl], preferred_element_type=F32)
            outs.append(o / l)
        o_ref[...] = jnp.concatenate(outs, axis=-1)


def _neighbourhood(q, k, v, tab, S):
    T = q.shape[0]
    rows = S // GRID_W
    n_ctx = T - S
    resident = lambda shape: pl.BlockSpec(shape, lambda i: (0,) * len(shape),
                                          pipeline_mode=pl.Buffered(1))
    return pl.pallas_call(
        functools.partial(_na_kernel, rows=rows, n_ctx=n_ctx),
        grid=(T // GRID_W,),
        in_specs=[pl.BlockSpec((GRID_W, 256), lambda i: (i, 0)),
                  resident((T, 256)), resident((T, 256)),
                  resident((N_HEADS, NA_KH, GRID_W, NA_KH * GRID_W))],
        out_specs=pl.BlockSpec((GRID_W, 256), lambda i: (i, 0)),
        out_shape=jax.ShapeDtypeStruct((T, 256), F32),
        compiler_params=_params(1), name="nbr_attn",
    )(q, k, v, tab)


def _lru_kernel(x_ref, xp_ref, xn_ref, cw_ref, cb_ref, wa_ref, ba_ref, wi_ref, bi_ref,
                lam_ref, y_ref, h_scr, *, reverse, n_lat, n_ctx):
    j = pl.program_id(0)
    tc = x_ref.shape[0]
    n_all = n_lat + n_ctx
    if reverse:
        chunk = jnp.where(j < n_ctx, n_all - 1 - j, n_all - 1 - j)
    else:
        chunk = jnp.where(j < n_ctx, n_lat + j, j - n_ctx)
    has_prev = jnp.logical_and(chunk != 0, chunk != n_lat)
    has_next = jnp.logical_and(chunk != n_lat - 1, chunk != n_all - 1)

    @pl.when(j == 0)
    def _():
        h_scr[...] = jnp.zeros(h_scr.shape, F32)

    xp = jnp.where(has_prev, xp_ref[...], 0.0)
    xn = jnp.where(has_next, xn_ref[...], 0.0)
    xe = jnp.concatenate([xp, x_ref[...], xn], axis=0)
    ne = tc + 16
    cw = cw_ref[...]
    xc = (cw[0:1] * pltpu.roll(xe, 1, 0)[8:8 + tc]
          + cw[1:2] * xe[8:8 + tc]
          + cw[2:3] * pltpu.roll(xe, ne - 1, 0)[8:8 + tc]
          + cw[3:4] * pltpu.roll(xe, ne - 2, 0)[8:8 + tc]
          + cb_ref[...])

    r = jax.nn.sigmoid(jnp.dot(xc, wa_ref[0], preferred_element_type=F32, precision=HIGHEST)
                       + ba_ref[0])
    gi = jax.nn.sigmoid(jnp.dot(xc, wi_ref[0], preferred_element_type=F32, precision=HIGHEST)
                        + bi_ref[0])
    z = -lam_ref[0]
    softplus = jnp.maximum(z, 0.0) + jnp.log1p(jnp.exp(-jnp.abs(z)))
    log_a = -LRU_C * r * softplus
    a = jnp.exp(log_a)
    b = jnp.sqrt(-jnp.tanh(log_a) * (a * a + 1.0)) * (gi * xc)

    rowi = lax.broadcasted_iota(jnp.int32, (tc, 1), 0)
    s = 1
    while s < tc:
        if reverse:
            keep = rowi < tc - s
            a_s = jnp.where(keep, pltpu.roll(a, tc - s, 0), 1.0)
            b_s = jnp.where(keep, pltpu.roll(b, tc - s, 0), 0.0)
        else:
            keep = rowi >= s
            a_s = jnp.where(keep, pltpu.roll(a, s, 0), 1.0)
            b_s = jnp.where(keep, pltpu.roll(b, s, 0), 0.0)
        b = a * b_s + b
        a = a * a_s
        s *= 2
    hcur = b + a * h_scr[...]
    y_ref[...] = hcur
    h_scr[...] = hcur[0:1] if reverse else hcur[tc - 1:tc]


def _lru_scan(lx, lw, S, tc, reverse):
    T = lx.shape[0]
    n_lat, n_ctx = S // tc, (T - S) // tc
    n_all = n_lat + n_ctx
    d = 1 if reverse else 0
    t8 = tc // 8

    def chunk_of(j):
        if reverse:
            return n_all - 1 - j
        return jnp.where(j < n_ctx, n_lat + j, j - n_ctx)

    dspec = lambda shape: pl.BlockSpec((1,) + shape, lambda j: (d,) + (0,) * len(shape))
    return pl.pallas_call(
        functools.partial(_lru_kernel, reverse=reverse, n_lat=n_lat, n_ctx=n_ctx),
        grid=(n_all,),
        in_specs=[pl.BlockSpec((tc, 256), lambda j: (chunk_of(j), 0)),
                  pl.BlockSpec((8, 256), lambda j: (jnp.maximum(chunk_of(j) * t8 - 1, 0), 0)),
                  pl.BlockSpec((8, 256),
                               lambda j: (jnp.minimum((chunk_of(j) + 1) * t8, T // 8 - 1), 0)),
                  _full((4, 256)), _full((1, 256)),
                  dspec((256, 256)), dspec((1, 256)), dspec((256, 256)), dspec((1, 256)),
                  dspec((1, 256))],
        out_specs=pl.BlockSpec((tc, 256), lambda j: (chunk_of(j), 0)),
        out_shape=jax.ShapeDtypeStruct((T, 256), F32),
        scratch_shapes=[pltpu.VMEM((1, 256), F32)],
        compiler_params=_params(1), name="rglru_bwd" if reverse else "rglru_fwd",
    )(lx, lx, lx, lw["conv_w"], lw["conv_b"], lw["wa"], lw["ba"], lw["wi"], lw["bi"], lw["lam"])


def _merge_kernel(x_ref, mod_ref, oa_ref, obl_ref, obc_ref, ocl_ref, occ_ref, yf_ref, yb_ref,
                  lg_ref, gg_ref, wout_ref, g2_ref, rw_ref, rb_ref, lt_ref, ut_ref,
                  x1_o, tok_o, slot_o, gate_o, cnt_o, *, n_lat_tiles):
    i = pl.program_id(0)
    tm = x_ref.shape[0]
    is_ctx = i >= n_lat_tiles

    gg = gg_ref[...]
    od = (yf_ref[...] + yb_ref[...]) * lg_ref[...]
    ob = jnp.where(is_ctx, obc_ref[...], obl_ref[...])
    oc = jnp.where(is_ctx, occ_ref[...], ocl_ref[...])
    parts = [oa_ref[...], ob, oc, od]
    ycat = jnp.concatenate(
        [(_rms(p) * gg[:, GROUP_W * n:GROUP_W * (n + 1)]).astype(BF16) for n, p in enumerate(parts)],
        axis=-1)
    y = jnp.dot(ycat, wout_ref[...], preferred_element_type=F32)
    gt1 = mod_ref[0, 2:3, :]
    sh2 = mod_ref[0, 3:4, :]
    sc2 = mod_ref[0, 4:5, :]
    x1 = x_ref[...] + gt1 * y
    x1_o[...] = x1
    tok = _rms(x1) * g2_ref[...] * (1.0 + sc2) + sh2
    tok_o[...] = tok.astype(BF16)

    logits = jnp.dot(tok, rw_ref[...], preferred_element_type=F32, precision=HIGHEST) + rb_ref[...]
    lane = lax.broadcasted_iota(jnp.int32, (tm, LANES), 1)
    v_acc = jnp.zeros((tm, LANES), F32)
    onehots = []
    v0 = None
    work = logits
    for k in range(TOP_K):
        vk = jnp.max(work, axis=-1, keepdims=True)
        ek = jnp.min(jnp.where(work == vk, lane, LANES), axis=-1, keepdims=True)
        sel = lane == ek
        onehots.append(sel)
        work = jnp.where(sel, NEG * 2.0, work)
        if k == 0:
            v0 = vk
        v_acc = jnp.where(lane == k, jnp.exp(vk - v0), v_acc)
    gate_o[...] = v_acc / jnp.sum(v_acc, axis=-1, keepdims=True)

    oh = [jnp.where(o, 1.0, 0.0) for o in onehots]
    oh_all = oh[0] + oh[1] + oh[2] + oh[3]
    cnt = jnp.sum(oh_all, axis=0, keepdims=True)
    run = jnp.floor((cnt + (RUN_ALIGN - 1.0)) * (1.0 / RUN_ALIGN)) * RUN_ALIGN
    run_start = jnp.dot(jnp.broadcast_to(run, (8, LANES)).astype(BF16), ut_ref[...],
                        preferred_element_type=F32)[0:1]
    excl = jnp.dot(lt_ref[...], oh_all.astype(BF16), preferred_element_type=F32) + run_start
    slot = jnp.full((tm, LANES), -1.0, F32)
    for k in range(TOP_K):
        sk = jnp.sum(oh[k] * excl, axis=-1, keepdims=True)
        slot = jnp.where(lane == k, sk, slot)
    slot_o[...] = slot.astype(jnp.int32)
    cnt_o[0] = jnp.broadcast_to(cnt, (8, LANES))


def _merge(x, mod, oa, ob_lat, ob_ctx, oc_lat, oc_ctx, yf, yb, lg, S, tm, consts, lw):
    T, D = x.shape
    nL = S // tm
    nt = T // tm
    row = lambda w: pl.BlockSpec((tm, w), lambda i: (i, 0))
    lat = pl.BlockSpec((tm, 256), lambda i: (jnp.minimum(i, nL - 1), 0))
    ctx = pl.BlockSpec((tm, 256), lambda i: (jnp.maximum(i - nL, 0), 0))
    sds = jax.ShapeDtypeStruct
    return pl.pallas_call(
        functools.partial(_merge_kernel, n_lat_tiles=nL), grid=(nt,),
        in_specs=[row(D), pl.BlockSpec((1, 6, D), lambda i: (jnp.where(i >= nL, 1, 0), 0, 0)),
                  row(256), lat, ctx, lat, ctx, row(256), row(256), row(256),
                  _full((1, D)), _full((D, D)), _full((1, D)), _full((D, LANES)),
                  _full((1, LANES)), _full((tm, tm)), _full((LANES, LANES))],
        out_specs=[row(D), row(D), row(LANES), row(LANES),
                   pl.BlockSpec((1, 8, LANES), lambda i: (i, 0, 0))],
        out_shape=[sds((T, D), F32), sds((T, D), BF16), sds((T, LANES), jnp.int32),
                   sds((T, LANES), F32), sds((nt, 8, LANES), F32)],
        compiler_params=_params(1), name="merge_router",
    )(x, mod, oa, ob_lat, ob_ctx, oc_lat, oc_ctx, yf, yb, lg, lw["grp_g"], lw["w_out"], lw["g2"],
      lw["router_w"], lw["router_b"], consts["lt"], consts["ut"])


def _routing_meta(cnt, bm, n_rows):
    c = cnt[:, 0, :N_EXPERTS].astype(jnp.int32)
    nt = c.shape[0]
    run = (c + RUN_ALIGN - 1) // RUN_ALIGN * RUN_ALIGN
    tile_off = jnp.cumsum(run, axis=0) - run
    tot = jnp.sum(run, axis=0)
    padded = (tot + bm - 1) // bm * bm
    seg_end = jnp.cumsum(padded)
    seg_start = seg_end - padded
    g_start = seg_start[None, :] + tile_off
    l_start = jnp.cumsum(run, axis=1) - run
    n_chunk = run // RUN_ALIGN
    n_tot = jnp.sum(n_chunk, axis=1, keepdims=True)
    meta = jnp.concatenate([g_start, l_start, n_chunk, n_tot,
                            jnp.zeros((nt, LANES - 3 * N_EXPERTS - 1), jnp.int32)], axis=1)
    n_zero = (padded - tot) // RUN_ALIGN
    zmeta = jnp.concatenate([seg_start + tot, n_zero, jnp.sum(n_zero)[None], seg_end[-1:] // bm,
                             jnp.zeros((LANES - 2 * N_EXPERTS - 2,), jnp.int32)])
    blk_row = jnp.arange(n_rows // bm, dtype=jnp.int32) * bm
    blk_expert = jnp.minimum(jnp.sum((seg_end[None, :] <= blk_row[:, None]).astype(jnp.int32), axis=1),
                             N_EXPERTS - 1)
    n_used = seg_end[-1:] // bm
    return (meta.reshape(nt, 1, LANES).astype(jnp.int32), zmeta.reshape(1, 1, LANES).astype(jnp.int32),
            blk_expert.astype(jnp.int32), n_used.astype(jnp.int32))


def _run_copies(meta_ref, make_copy):
    def per_expert(e, c):
        g = meta_ref[0, 0, e]
        l = meta_ref[0, 0, N_EXPERTS + e]

        def issue(cc, c2):
            make_copy(pl.multiple_of(g + RUN_ALIGN * cc, RUN_ALIGN),
                      pl.multiple_of(l + RUN_ALIGN * cc, RUN_ALIGN)).start()
            return c2

        return lax.fori_loop(0, meta_ref[0, 0, 2 * N_EXPERTS + e], issue, c)

    lax.fori_loop(0, N_EXPERTS, per_expert, 0)


def _drain(n, make_copy):
    def body(_, c):
        make_copy(0, 0).wait()
        return c

    lax.fori_loop(0, n, body, 0)


def _dispatch_kernel(meta_ref, zmeta_ref, slot_ref, tok_ref, xin_ref, xs_scr, z_scr, sem):
    i = pl.program_id(0)
    tm = tok_ref.shape[0]
    n_slot = xs_scr.shape[0]
    slot_t = jnp.transpose(slot_ref[...].astype(F32))
    srow = lax.broadcasted_iota(jnp.int32, (n_slot, tm), 0).astype(F32)
    perm = jnp.zeros((n_slot, tm), F32)
    for k in range(TOP_K):
        perm = perm + jnp.where(srow == slot_t[k:k + 1, :], 1.0, 0.0)
    xs_scr[...] = jnp.dot(perm.astype(BF16), tok_ref[...], preferred_element_type=F32)

    def to_global(g_row, l_row):
        return pltpu.make_async_copy(xs_scr.at[pl.ds(l_row, RUN_ALIGN)],
                                     xin_ref.at[pl.ds(g_row, RUN_ALIGN)], sem)

    _run_copies(meta_ref, to_global)
    _drain(meta_ref[0, 0, 3 * N_EXPERTS], to_global)

    @pl.when(i == pl.num_programs(0) - 1)
    def _():
        z_scr[...] = jnp.zeros(z_scr.shape, F32)

        def zero_copy(g_row):
            return pltpu.make_async_copy(z_scr.at[pl.ds(0, RUN_ALIGN)],
                                         xin_ref.at[pl.ds(g_row, RUN_ALIGN)], sem)

        def per_expert(e, c):
            g = zmeta_ref[0, 0, e]

            def issue(cc, c2):
                zero_copy(pl.multiple_of(g + RUN_ALIGN * cc, RUN_ALIGN)).start()
                return c2

            return lax.fori_loop(0, zmeta_ref[0, 0, N_EXPERTS + e], issue, c)

        lax.fori_loop(0, N_EXPERTS, per_expert, 0)

        def drain(_, c):
            zero_copy(0).wait()
            return c

        lax.fori_loop(0, zmeta_ref[0, 0, 2 * N_EXPERTS], drain, 0)

        bm = z_scr.shape[0]

        def block_copy(b):
            return pltpu.make_async_copy(z_scr, xin_ref.at[pl.ds(pl.multiple_of(b * bm, bm), bm)], sem)

        def issue_block(b, c):
            block_copy(b).start()
            return c

        def drain_block(b, c):
            block_copy(b).wait()
            return c

        n_used = zmeta_ref[0, 0, 2 * N_EXPERTS + 1]
        lax.fori_loop(n_used, xin_ref.shape[0] // bm, issue_block, 0)
        lax.fori_loop(n_used, xin_ref.shape[0] // bm, drain_block, 0)


def _dispatch(tok, slot, meta, zmeta, n_rows, tm, bm):
    T, D = tok.shape
    n_slot = TOP_K * tm + N_EXPERTS * RUN_ALIGN
    smem = lambda idx: pl.BlockSpec((1, 1, LANES), idx, memory_space=pltpu.SMEM)
    return pl.pallas_call(
        _dispatch_kernel,
        grid=(T // tm,),
        in_specs=[smem(lambda i: (i, 0, 0)), smem(lambda i: (0, 0, 0)),
                  pl.BlockSpec((tm, LANES), lambda i: (i, 0)),
                  pl.BlockSpec((tm, D), lambda i: (i, 0))],
        out_specs=pl.BlockSpec(memory_space=pl.ANY),
        out_shape=jax.ShapeDtypeStruct((n_rows, D), F32),
        scratch_shapes=[pltpu.VMEM((n_slot, D), F32), pltpu.VMEM((bm, D), F32),
                        pltpu.SemaphoreType.DMA(())],
        compiler_params=_params(1), name="moe_dispatch",
    )(meta, zmeta, slot, tok)


def _expert_kernel(be_ref, nu_ref, x_ref, wgu_ref, bgu_ref, wd_ref, bd_ref, y_ref):
    i = pl.program_id(0)
    d_e = wd_ref.shape[1]

    @pl.when(i < nu_ref[0])
    def _():
        gu = jnp.dot(x_ref[...].astype(BF16), wgu_ref[0], preferred_element_type=F32) + bgu_ref[0]
        x_glu = jnp.minimum(gu[:, :d_e], SWIGLU_LIMIT)
        x_lin = jnp.clip(gu[:, d_e:], -SWIGLU_LIMIT, SWIGLU_LIMIT)
        act = x_glu * jax.nn.sigmoid(SWIGLU_ALPHA * x_glu) * (x_lin + 1.0)
        y_ref[...] = jnp.dot(act.astype(BF16), wd_ref[0], preferred_element_type=F32) + bd_ref[0]

    @pl.when(i >= nu_ref[0])
    def _():
        y_ref[...] = jnp.zeros(y_ref.shape, F32)


def _experts(xin, blk_expert, n_used, w_gu, b_gu, w_down, b_down, bm):
    P, D = xin.shape
    E, _, d2 = w_gu.shape
    d_e = d2 // 2
    blk = lambda i, be, nu: jnp.minimum(i, nu[0] - 1)
    grid_spec = pltpu.PrefetchScalarGridSpec(
        num_scalar_prefetch=2, grid=(P // bm,),
        in_specs=[pl.BlockSpec((bm, D), lambda i, be, nu: (blk(i, be, nu), 0)),
                  pl.BlockSpec((1, D, d2), lambda i, be, nu: (be[blk(i, be, nu)], 0, 0)),
                  pl.BlockSpec((1, 1, d2), lambda i, be, nu: (be[blk(i, be, nu)], 0, 0)),
                  pl.BlockSpec((1, d_e, D), lambda i, be, nu: (be[blk(i, be, nu)], 0, 0)),
                  pl.BlockSpec((1, 1, D), lambda i, be, nu: (be[blk(i, be, nu)], 0, 0))],
        out_specs=pl.BlockSpec((bm, D), lambda i, be, nu: (i, 0)))
    return pl.pallas_call(
        _expert_kernel, grid_spec=grid_spec,
        out_shape=jax.ShapeDtypeStruct((P, D), F32),
        compiler_params=_params(1), name="moe_experts",
    )(blk_expert, n_used, xin, w_gu, b_gu.reshape(E, 1, d2), w_down, b_down.reshape(E, 1, D))


def _combine_kernel(meta_ref, slot_ref, gate_ref, y_ref, x_ref, mod_ref, o_ref, ybuf, sem):
    tm = x_ref.shape[0]
    n_slot, d = ybuf.shape

    def to_local(g_row, l_row):
        return pltpu.make_async_copy(y_ref.at[pl.ds(g_row, RUN_ALIGN)],
                                     ybuf.at[pl.ds(l_row, RUN_ALIGN)], sem)

    _run_copies(meta_ref, to_local)
    n_tot = meta_ref[0, 0, 3 * N_EXPERTS]

    def zero_tail(cc, c):
        ybuf[pl.ds(pl.multiple_of(RUN_ALIGN * cc, RUN_ALIGN), RUN_ALIGN), :] = jnp.zeros(
            (RUN_ALIGN, d), F32)
        return c

    lax.fori_loop(n_tot, n_slot // RUN_ALIGN, zero_tail, 0)
    _drain(n_tot, to_local)

    slot = slot_ref[...]
    gate = gate_ref[...]
    col = lax.broadcasted_iota(jnp.int32, (tm, n_slot), 1)
    w = jnp.zeros((tm, n_slot), F32)
    for k in range(TOP_K):
        w = w + jnp.where(col == slot[:, k:k + 1], gate[:, k:k + 1], 0.0)
    w_hi = w.astype(BF16)
    w_lo = (w - w_hi.astype(F32)).astype(BF16)
    yb = ybuf[...].astype(BF16)
    f = (jnp.dot(w_hi, yb, preferred_element_type=F32)
         + jnp.dot(w_lo, yb, preferred_element_type=F32))
    o_ref[...] = x_ref[...] + mod_ref[0, 5:6, :] * f


def _combine(y, slot, gate, meta, x1, mod, S, tm):
    T, D = x1.shape
    nL = S // tm
    n_slot = TOP_K * tm + N_EXPERTS * RUN_ALIGN
    return pl.pallas_call(
        _combine_kernel,
        grid=(T // tm,),
        in_specs=[pl.BlockSpec((1, 1, LANES), lambda i: (i, 0, 0), memory_space=pltpu.SMEM),
                  pl.BlockSpec((tm, LANES), lambda i: (i, 0)),
                  pl.BlockSpec((tm, LANES), lambda i: (i, 0)),
                  pl.BlockSpec(memory_space=pl.ANY),
                  pl.BlockSpec((tm, D), lambda i: (i, 0)),
                  pl.BlockSpec((1, 6, D), lambda i: (jnp.where(i >= nL, 1, 0), 0, 0))],
        out_specs=pl.BlockSpec((tm, D), lambda i: (i, 0)),
        out_shape=jax.ShapeDtypeStruct((T, D), F32),
        scratch_shapes=[pltpu.VMEM((n_slot, D), F32), pltpu.SemaphoreType.DMA(())],
        compiler_params=_params(1), name="moe_combine",
    )(meta, slot, gate, y, x1, mod)


def _rope_tables(S, T, rot_dim, width, lane0):
    t = jnp.arange(S)
    rowf = (t // GRID_W).astype(F32)
    colf = (t % GRID_W).astype(F32)
    ax = rot_dim // 2
    inv = ROPE_THETA ** (-jnp.arange(0, ax, 2, dtype=F32) / ax)
    ang = jnp.concatenate([rowf[:, None] * inv, colf[:, None] * inv], axis=-1)
    cos, sin = jnp.cos(ang), jnp.sin(ang)
    h = rot_dim // 4
    zero = jnp.zeros((S, h), F32)
    cos_f = jnp.concatenate([cos[:, :h], cos[:, :h], cos[:, h:], cos[:, h:]], axis=-1)
    s_lo = jnp.concatenate([zero, sin[:, :h], zero, sin[:, h:]], axis=-1)
    s_hi = jnp.concatenate([-sin[:, :h], zero, -sin[:, h:], zero], axis=-1)

    def place(tab, fill):
        out = jnp.full((T, width), fill, F32)
        return out.at[:S, lane0:lane0 + rot_dim].set(tab)

    return place(cos_f, 1.0), place(s_lo, 0.0), place(s_hi, 0.0)


def _constants(S, T, tm):
    cq, slq, shq = _rope_tables(S, T, HEAD_DIM, HEAD_DIM, 0)
    cq, slq, shq = (jnp.tile(a, (1, N_HEADS)) for a in (cq, slq, shq))
    cm, slm, shm = _rope_tables(S, T, MLA_ROPE, MLA_PAD, MLA_NOPE)
    bd64 = np.kron(np.eye(4, dtype=np.float32), np.ones((64, 64), np.float32))
    sizes = [MLA_NOPE, MLA_ROPE, MLA_PAD - MLA_NOPE - MLA_ROPE] * N_HEADS
    gid = np.repeat(np.arange(len(sizes)), sizes)
    gm = (gid[:, None] == gid[None, :]).astype(np.float32)
    invn = (1.0 / np.repeat(np.asarray(sizes, np.float32), sizes))[None, :]
    lt = np.tril(np.ones((tm, tm), np.float32), -1)
    ut = np.triu(np.ones((LANES, LANES), np.float32), 1)
    return dict(cq=cq, slq=slq, shq=shq, cm=cm, slm=slm, shm=shm,
                bd64=jnp.asarray(bd64, BF16), gm=jnp.asarray(gm, BF16),
                invn=jnp.asarray(invn), lt=jnp.asarray(lt, BF16), ut=jnp.asarray(ut, BF16))


def _na_bias_tables(rpb):
    qc = np.arange(GRID_W)[:, None]
    kc = np.arange(GRID_W)[None, :]
    c0 = np.clip(qc - NA_KW // 2, 0, GRID_W - NA_KW)
    valid = (kc >= c0) & (kc < c0 + NA_KW)
    sel = ((kc - qc + NA_KW - 1)[:, :, None] == np.arange(2 * NA_KW - 1)) & valid[:, :, None]
    toep = jnp.einsum("lhab,qkb->lhaqk", rpb, jnp.asarray(sel, F32), precision=HIGHEST)
    toep = jnp.where(jnp.asarray(valid)[None, None, None], toep, NEG)
    L, H = rpb.shape[:2]
    tabs = []
    for off in range(NA_KH):
        rows = toep[:, :, NA_KH - 1 - off:2 * NA_KH - 1 - off]
        tabs.append(rows.transpose(0, 1, 3, 2, 4).reshape(L, H, GRID_W, NA_KH * GRID_W))
    return jnp.stack(tabs, axis=2)


def _block_diag(w):
    n, bw, _ = w.shape
    eye = jnp.eye(n, dtype=w.dtype)
    return (eye[:, None, :, None] * w[:, :, None, :]).reshape(n * bw, n * bw)


def _layer_weights(l, p):
    D = p["w_in"].shape[1]
    w_in = p["w_in"][l]
    offs = np.cumsum([0, 256, 256, 256, 256, 128, 128, 256, 128, 32, 256, 256])
    seg = lambda n: w_in[:, offs[n]:offs[n + 1]]
    z = lambda n: jnp.zeros((D, n), F32)
    w_in_r = jnp.concatenate(
        [seg(0), seg(1), seg(2), seg(3), seg(4), seg(5), seg(6), seg(7), seg(9), seg(10),
         z(MLA_NOPE), seg(8), z(MLA_PAD - MLA_NOPE - MLA_ROPE)], axis=-1).astype(BF16)
    sc = HEAD_DIM ** -0.5
    sc_m = (MLA_NOPE + MLA_ROPE) ** -0.5
    t4 = lambda g: jnp.tile(g, N_HEADS)[None, :]
    wuq = p["mla_wuq"][l].reshape(-1, N_HEADS, MLA_NOPE + MLA_ROPE)
    wuq = jnp.pad(wuq, ((0, 0), (0, 0), (0, MLA_PAD - MLA_NOPE - MLA_ROPE)))
    wukv = p["mla_wukv"][l].reshape(-1, N_HEADS, MLA_NOPE + HEAD_DIM)
    wukv_k = jnp.pad(wukv[:, :, :MLA_NOPE], ((0, 0), (0, 0), (0, MLA_PAD - MLA_NOPE)))
    qn, kn = p["mla_qn"][l], p["mla_kn"][l]
    padq = jnp.pad(qn * (sc_m * LOG2E), (0, MLA_PAD - MLA_NOPE - MLA_ROPE))
    padkn = jnp.pad(kn[:MLA_NOPE], (0, MLA_PAD - MLA_NOPE))
    padkr = jnp.pad(kn[MLA_NOPE:], (MLA_NOPE, MLA_PAD - MLA_NOPE - MLA_ROPE))
    rw =jnp.pad(p["router_w"][l], ((0, 0), (0, LANES - N_EXPERTS)))
    rb = jnp.pad(p["router_b"][l], (0, LANES - N_EXPERTS), constant_values=NEG)[None, :]
    return dict(
        g1=p["norm1_g"][l][None, :], g2=p["norm2_g"][l][None, :], w_in=w_in_r,
        na_qn=t4(p["na_qn"][l] * sc), na_kn=t4(p["na_kn"][l]),
        gqa_qn=t4(p["gqa_qn"][l] * (sc * LOG2E)), gqa_kn=jnp.tile(p["gqa_kn"][l], GQA_KV_HEADS)[None, :],
        qa_g=p["mla_qa_g"][l][None, :], kva_g=p["mla_kva_g"][l][None, :],
        mla_qn=t4(padq), mla_knn=t4(padkn), mla_knr=padkr[None, :],
        wuq=wuq.reshape(-1, N_HEADS * MLA_PAD).astype(BF16),
        wukv_k=wukv_k.reshape(-1, N_HEADS * MLA_PAD).astype(BF16),
        wukv_v=wukv[:, :, MLA_NOPE:].reshape(-1, N_HEADS * HEAD_DIM).astype(BF16),
        conv_w=p["lru_conv_w"][l][:, 0, :], conv_b=p["lru_conv_b"][l][None, :],
        wa=jnp.stack([_block_diag(p["lru_wa"][l][d]) for d in range(2)]),
        wi=jnp.stack([_block_diag(p["lru_wi"][l][d]) for d in range(2)]),
        ba=p["lru_ba"][l][:, None, :], bi=p["lru_bi"][l][:, None, :],
        lam=p["lru_lam"][l][:, None, :],
        grp_g=p["grp_g"][l][None, :], w_out=p["w_out"][l].astype(BF16),
        router_w=rw, router_b=rb,
        w_gu=p["exp_w_gu"][l].astype(BF16), b_gu=p["exp_b_gu"][l],
        w_down=p["exp_w_down"][l].astype(BF16), b_down=p["exp_b_down"][l],
    )


def _tiles(S, C):
    tm = min(256, C)
    tq = min(512, S)
    T = S + C
    tk = next(t for t in (1280, 640, 256, 128) if T % t == 0)
    return dict(tm=tm, tq=tq, tk=tk, tc=min(256, C), bm=256, tt=min(128, C))


def kernel(x, c, ctx, c_ctx, ada_w, ada_b, norm1_g, norm2_g, w_in, na_qn, na_kn, na_rpb, gqa_qn, gqa_kn, mla_qa_g, mla_kva_g, mla_wuq, mla_wukv, mla_qn, mla_kn, lru_conv_w, lru_conv_b, lru_wa, lru_ba, lru_wi, lru_bi, lru_lam, grp_g, w_out, router_w, router_b, exp_w_gu, exp_b_gu, exp_w_down, exp_b_down):
    p = dict(norm1_g=norm1_g, norm2_g=norm2_g, w_in=w_in, na_qn=na_qn, na_kn=na_kn, na_rpb=na_rpb,
             gqa_qn=gqa_qn, gqa_kn=gqa_kn, mla_qa_g=mla_qa_g, mla_kva_g=mla_kva_g,
             mla_wuq=mla_wuq, mla_wukv=mla_wukv, mla_qn=mla_qn, mla_kn=mla_kn,
             lru_conv_w=lru_conv_w, lru_conv_b=lru_conv_b, lru_wa=lru_wa, lru_ba=lru_ba,
             lru_wi=lru_wi, lru_bi=lru_bi, lru_lam=lru_lam, grp_g=grp_g, w_out=w_out,
             router_w=router_w, router_b=router_b, exp_w_gu=exp_w_gu, exp_b_gu=exp_b_gu,
             exp_w_down=exp_w_down, exp_b_down=exp_b_down)
    B, S, D = x.shape
    assert B == 1 and S % GRID_W == 0
    C = ctx.shape[1]
    T = S + C
    L = ada_w.shape[0]
    ts = _tiles(S, C)
    tm, bm = ts["tm"], ts["bm"]
    consts = _constants(S, T, tm)

    c8 = jnp.zeros((8, D), F32).at[0].set(c[0]).at[1].set(c_ctx)
    mods = _modulation(c8, ada_w, ada_b)[:, :2].reshape(L, 2, 6, D)

    n_rows = -(-(T * TOP_K + (T // tm) * N_EXPERTS * (RUN_ALIGN - 1) + N_EXPERTS * (bm - 1)) // bm) * bm
    na_tabs = _na_bias_tables(na_rpb)
    tq, tk = ts["tq"], ts["tk"]
    xs = jnp.concatenate([x[0], ctx[0]], axis=0)
    for l in range(L):
        lw = _layer_weights(l, p)
        mod = mods[l]
        (naq, nak, nav, gq, gk, gv, mq, mk, mv, lx, lg) = _premix(xs, mod, S, tm, consts, lw)
        oa = _neighbourhood(naq, nak, nav, na_tabs[l], S)
        ob_lat = _flash(gq, gk, gv, tq, tk, S // tq, 0, T // tk, 0)
        ob_ctx = _flash(gq, gk, gv, C, C, 1, S // C, 1, S // C)
        oc_lat = _flash(mq, mk, mv, tq, tk, S // tq, 0, T // tk, 0)
        oc_ctx = _flash(mq, mk, mv, C, C, 1, S // C, 1, S // C)
        yf = _lru_scan(lx, lw, S, ts["tc"], False)
        yb = _lru_scan(lx, lw, S, ts["tc"], True)
        x1, tok, slot, gate, cnt = _merge(xs, mod, oa, ob_lat, ob_ctx, oc_lat, oc_ctx, yf, yb, lg,
                                          S, tm, consts, lw)
        meta, zmeta, blk_expert, n_used = _routing_meta(cnt, bm, n_rows)
        xin = _dispatch(tok, slot, meta, zmeta, n_rows, tm, bm)
        y = _experts(xin, blk_expert, n_used, lw["w_gu"], lw["b_gu"], lw["w_down"], lw["b_down"], bm)
        xs = _combine(y, slot, gate, meta, x1, mod, S, tm)
    return xs[:S][None]
```

```python
import functools

import numpy as np
import jax
import jax.numpy as jnp
from jax import lax
from jax.experimental import pallas as pl
from jax.experimental.pallas import tpu as pltpu

F32 = jnp.float32
BF16 = jnp.bfloat16
HIGHEST = lax.Precision.HIGHEST

GRID_W = 64
HEAD_DIM = 64
N_HEADS = 4
GROUP_W = 256
GQA_KV_HEADS = 2
NA_KH = 8
NA_KW = 16
MLA_NOPE = 64
MLA_ROPE = 32
MLA_PAD = 128
LRU_C = 8.0
N_EXPERTS = 32
TOP_K = 4
SWIGLU_LIMIT = 7.0
SWIGLU_ALPHA = 1.702
ROPE_THETA = 10000.0
EPS = 1e-6
NEG = -1e30
LANES = 128
FLASH_ROW_BLOCK = 256
LOG2E = 1.4426950408889634
RUN_ALIGN = 8
CHUNK_ROWS = (32, 16, 8)
VMEM_LIMIT = 56 * 1024 * 1024

_C_NAQ, _C_NAK, _C_NAV = 0, 256, 512
_C_GQ, _C_GK, _C_GV = 768, 1024, 1152
_C_MCQ, _C_MCKV = 1280, 1536
_C_LX, _C_LG = 1664, 1920
_C_MKR = 2176
IN_COLS_R = 2304


def _params(n_axes, vmem=VMEM_LIMIT):
    return pltpu.CompilerParams(dimension_semantics=("arbitrary",) * n_axes,
                                vmem_limit_bytes=vmem)


def _full(shape):
    n = len(shape)
    return pl.BlockSpec(shape, lambda *_: (0,) * n)


def _mod_kernel(c_ref, w_ref, b_ref, o_ref):
    cc = c_ref[...]
    s = cc * jax.nn.sigmoid(cc)
    o_ref[0] = jnp.dot(s, w_ref[0], preferred_element_type=F32, precision=HIGHEST) + b_ref[0]


def _modulation(c8, ada_w, ada_b):
    L, D, D6 = ada_w.shape
    tn = D6 // 6
    return pl.pallas_call(
        _mod_kernel,
        grid=(L, D6 // tn),
        in_specs=[pl.BlockSpec((8, D), lambda l, j: (0, 0)),
                  pl.BlockSpec((1, D, tn), lambda l, j: (l, 0, j)),
                  pl.BlockSpec((1, 1, tn), lambda l, j: (l, 0, j))],
        out_specs=pl.BlockSpec((1, 8, tn), lambda l, j: (l, 0, j)),
        out_shape=jax.ShapeDtypeStruct((L, 8, D6), F32),
        compiler_params=_params(2),
        name="adaln_mod",
    )(c8, ada_w, ada_b.reshape(L, 1, D6))


def _rms(z):
    return z * lax.rsqrt(jnp.mean(z * z, axis=-1, keepdims=True) + EPS)


def _group_rms(z, gmat, inv_n):
    zz = z * z
    hi = zz.astype(BF16)
    lo = (zz - hi.astype(F32)).astype(BF16)
    ss = (jnp.dot(hi, gmat, preferred_element_type=F32)
          + jnp.dot(lo, gmat, preferred_element_type=F32))
    return z * lax.rsqrt(ss * inv_n + EPS)


def _rope(z, cos, sin_lo, sin_hi, half):
    w = z.shape[-1]
    return z * cos + pltpu.roll(z, half, 1) * sin_lo + pltpu.roll(z, w - half, 1) * sin_hi


def _tile4(t):
    return jnp.concatenate([t, t, t, t], axis=-1)


def _rope_tile(row_ref, col_ref, width):
    rt = row_ref[0]
    rows = jnp.concatenate([jnp.broadcast_to(rt[g:g + 1], (GRID_W, rt.shape[1]))
                            for g in range(rt.shape[0])], axis=0)
    tab = rows + col_ref[0]
    return tab[:, :width], tab[:, width:2 * width], tab[:, 2 * width:]


def _premix_kernel(x_ref, mod_ref, g1_ref, win_ref, bd_ref, gm_ref,
                   rq_ref, cq_ref, rm_ref, cm_ref,
                   naqn_ref, nakn_ref, gqn_ref, gkn_ref, qag_ref, kvag_ref,
                   mqn_ref, mknn_ref, mknr_ref, invn_ref, wuq_ref, wukvk_ref, wukvv_ref,
                   naq_o, nak_o, nav_o, gq_o, gk_o, gv_o, mq_o, mk_o, mv_o, lx_o, lg_o):
    tm = x_ref.shape[0]
    xt = x_ref[...]
    sh1 = mod_ref[0, 0:1, :]
    sc1 = mod_ref[0, 1:2, :]
    h = _rms(xt) * g1_ref[...] * (1.0 + sc1) + sh1
    u = jnp.dot(h.astype(BF16), win_ref[...], preferred_element_type=F32)

    bd = bd_ref[...]
    inv64 = 1.0 / HEAD_DIM
    ones64 = jnp.ones((tm, HEAD_DIM), F32)

    naq_o[...] = (_group_rms(u[:, _C_NAQ:_C_NAQ + 256], bd, inv64) * naqn_ref[...]).astype(BF16)
    nak_o[...] = (_group_rms(u[:, _C_NAK:_C_NAK + 256], bd, inv64) * nakn_ref[...]).astype(BF16)
    nav_o[...] = u[:, _C_NAV:_C_NAV + 256].astype(BF16)

    cq, slq, shq = _rope_tile(rq_ref, cq_ref, 256)
    gq = _rope(_group_rms(u[:, _C_GQ:_C_GQ + 256], bd, inv64) * gqn_ref[...], cq, slq, shq, 16)
    for hh in range(N_HEADS):
        gq_o[hh] = gq[:, 64 * hh:64 * hh + 64].astype(BF16)
    gk = _rope(_group_rms(u[:, _C_GK:_C_GK + 128], bd[:128, :128], inv64) * gkn_ref[...],
               cq[:, :128], slq[:, :128], shq[:, :128], 16)
    gv = u[:, _C_GV:_C_GV + 128]
    for hh in range(GQA_KV_HEADS):
        gk_o[hh] = gk[:, 64 * hh:64 * hh + 64].astype(BF16)
        gv_o[hh] = jnp.concatenate([gv[:, 64 * hh:64 * hh + 64], ones64], axis=-1).astype(BF16)

    gm = gm_ref[...]
    invn = invn_ref[...]
    cm, slm, shm = _rope_tile(rm_ref, cm_ref, MLA_PAD)
    cqn =_rms(u[:, _C_MCQ:_C_MCQ + 256]) * qag_ref[...]
    mq = jnp.dot(cqn.astype(BF16), wuq_ref[...], preferred_element_type=F32)
    mq = _group_rms(mq, gm, invn) * mqn_ref[...]
    mq = _rope(mq, _tile4(cm), _tile4(slm), _tile4(shm), 8)
    ckvn = (_rms(u[:, _C_MCKV:_C_MCKV + 128]) * kvag_ref[...]).astype(BF16)
    mkn = jnp.dot(ckvn, wukvk_ref[...], preferred_element_type=F32)
    mkn = _group_rms(mkn, gm, invn) * mknn_ref[...]
    mvv = jnp.dot(ckvn, wukvv_ref[...], preferred_element_type=F32)
    kr = _group_rms(u[:, _C_MKR:_C_MKR + 128], gm[:128, :128], invn[:, :128]) * mknr_ref[...]
    kr = _rope(kr, cm, slm, shm, 8)
    for hh in range(N_HEADS):
        mq_o[hh] = mq[:, 128 * hh:128 * hh + 128].astype(BF16)
        mk_o[hh] = (mkn[:, 128 * hh:128 * hh + 128] + kr).astype(BF16)
        mv_o[hh] = jnp.concatenate([mvv[:, 64 * hh:64 * hh + 64], ones64], axis=-1).astype(BF16)

    lx_o[...] = u[:, _C_LX:_C_LX + 256]
    g = u[:, _C_LG:_C_LG + 256]
    lg_o[...] = 0.5 * g * (1.0 + jnp.tanh(0.7978845608028654 * (g + 0.044715 * g * g * g)))


def _premix(x, mod, S, tm, consts, lw):
    T, D = x.shape
    nL = S // tm
    row = lambda w: pl.BlockSpec((tm, w), lambda i: (i, 0))
    hm = lambda hn, w: pl.BlockSpec((hn, tm, w), lambda i: (0, i, 0))
    in_specs = [
        row(D),
        pl.BlockSpec((1, 6, D), lambda i: (jnp.where(i >= nL, 1, 0), 0, 0)),
        _full((1, D)), _full((D, IN_COLS_R)), _full((256, 256)), _full((512, 512)),
        pl.BlockSpec((1, tm // GRID_W, 768), lambda i: (i, 0, 0)),
        pl.BlockSpec((1, tm, 768), lambda i: (jnp.where(i >= nL, 1, 0), 0, 0)),
        pl.BlockSpec((1, tm // GRID_W, 3 * MLA_PAD), lambda i: (i, 0, 0)),
        pl.BlockSpec((1, tm, 3 * MLA_PAD), lambda i: (jnp.where(i >= nL, 1, 0), 0, 0)),
        _full((1, 256)), _full((1, 256)), _full((1, 256)), _full((1, 128)),
        _full((1, 256)), _full((1, 128)),
        _full((1, 512)), _full((1, 512)), _full((1, 128)), _full((1, 512)),
        _full((256, 512)), _full((128, 512)), _full((128, 256)),
    ]
    out_specs = [row(256), row(256), row(256),
                 hm(4, 64), hm(2, 64), hm(2, 128),
                 hm(4, 128), hm(4, 128), hm(4, 128),
                 row(256), row(256)]
    sds = jax.ShapeDtypeStruct
    out_shape = [sds((T, 256), BF16), sds((T, 256), BF16), sds((T, 256), BF16),
                 sds((4, T, 64), BF16), sds((2, T, 64), BF16), sds((2, T, 128), BF16),
                 sds((4, T, 128), BF16), sds((4, T, 128), BF16), sds((4, T, 128), BF16),
                 sds((T, 256), F32), sds((T, 256), F32)]
    return pl.pallas_call(
        _premix_kernel, grid=(T // tm,), in_specs=in_specs, out_specs=out_specs,
        out_shape=out_shape, compiler_params=_params(1), name="premix",
    )(x, mod, lw["g1"], lw["w_in"], consts["bd64"], consts["gm"],
      consts["rope_q_row"], consts["rope_q_col"], consts["rope_m_row"], consts["rope_m_col"],
      lw["na_qn"], lw["na_kn"], lw["gqa_qn"], lw["gqa_kn"], lw["qa_g"], lw["kva_g"],
      lw["mla_qn"], lw["mla_knn"], lw["mla_knr"], consts["invn"],
      lw["wuq"], lw["wukv_k"], lw["wukv_v"])


def _flash_kernel(q_ref, k_ref, v_ref, o_ref, m_scr, acc_scr, *, group, rb):
    j = pl.program_id(1)
    n_kv = k_ref.shape[0]
    tq = q_ref.shape[1]

    @pl.when(j == 0)
    def _():
        m_scr[...] = jnp.full(m_scr.shape, NEG, F32)
        acc_scr[...] = jnp.zeros(acc_scr.shape, F32)

    chains = [(hq, slice(r0, r0 + rb)) for hq in range(n_kv * group) for r0 in range(0, tq, rb)]

    def scores(chain):
        hq, rows = chain
        return lax.dot_general(q_ref[hq, rows, :], k_ref[hq // group], (((1,), (1,)), ((), ())),
                               preferred_element_type=F32)

    s = scores(chains[0])
    for n, (hq, rows) in enumerate(chains):
        s_next = scores(chains[n + 1]) if n + 1 < len(chains) else None
        m_old = m_scr[hq, rows, :]
        m_new = jnp.maximum(m_old, jnp.max(s, axis=-1, keepdims=True))
        alpha = jnp.exp2(m_old - m_new)
        p = jnp.exp2(s - m_new).astype(BF16)
        acc_scr[hq, rows, :] = (alpha * acc_scr[hq, rows, :]
                                + jnp.dot(p, v_ref[hq // group], preferred_element_type=F32))
        m_scr[hq, rows, :] = m_new
        s = s_next

    @pl.when(j == pl.num_programs(1) - 1)
    def _():
        outs = []
        for hq in range(n_kv * group):
            a = acc_scr[hq]
            outs.append(a[:, :HEAD_DIM] / a[:, HEAD_DIM:HEAD_DIM + 1])
        o_ref[...] = jnp.concatenate(outs, axis=-1)


def _flash(q, k, v, tq, tk, n_q, q_blk0, n_kv, kv_blk0):
    hq, _, d = q.shape
    hk = k.shape[0]
    group = hq // hk
    return pl.pallas_call(
        functools.partial(_flash_kernel, group=group, rb=min(FLASH_ROW_BLOCK, tq)),
        grid=(n_q, n_kv),
        in_specs=[pl.BlockSpec((hq, tq, d), lambda i, j: (0, q_blk0 + i, 0)),
                  pl.BlockSpec((hk, tk, d), lambda i, j: (0, kv_blk0 + j, 0)),
                  pl.BlockSpec((hk, tk, 128), lambda i, j: (0, kv_blk0 + j, 0))],
        out_specs=pl.BlockSpec((tq, hq * HEAD_DIM), lambda i, j: (i, 0)),
        out_shape=jax.ShapeDtypeStruct((n_q * tq, hq * HEAD_DIM), F32),
        scratch_shapes=[pltpu.VMEM((hq, tq, 1), F32), pltpu.VMEM((hq, tq, 128), F32)],
        compiler_params=_params(2), name="flash_attn",
    )(q, k, v)


def _na_kernel(q_ref, k_ref, v_ref, tab_ref, o_ref, *, rows, n_ctx):
    i = pl.program_id(0)
    s_lat = rows * GRID_W
    kc = k_ref[pl.ds(s_lat, n_ctx), :]
    vc = v_ref[pl.ds(s_lat, n_ctx), :]
    q = q_ref[...]
    nt = (((1,), (1,)), ((), ()))

    @pl.when(i < rows)
    def _():
        r0 = jnp.clip(i - NA_KH // 2, 0, rows - NA_KH)
        off = i - r0
        start = pl.multiple_of(r0 * GRID_W, GRID_W)
        ks = k_ref[pl.ds(start, NA_KH * GRID_W), :]
        vs = v_ref[pl.ds(start, NA_KH * GRID_W), :]
        outs = []
        for h in range(N_HEADS):
            sl = slice(HEAD_DIM * h, HEAD_DIM * (h + 1))
            qh = q[:, sl]
            s_nb = lax.dot_general(qh, ks[:, sl], nt, preferred_element_type=F32) + tab_ref[h, off]
            s_cx = lax.dot_general(qh, kc[:, sl], nt, preferred_element_type=F32)
            m = jnp.maximum(jnp.max(s_nb, axis=-1, keepdims=True),
                            jnp.max(s_cx, axis=-1, keepdims=True))
            p_nb = jnp.exp(s_nb - m)
            p_cx = jnp.exp(s_cx - m)
            l = jnp.sum(p_nb, axis=-1, keepdims=True) + jnp.sum(p_cx, axis=-1, keepdims=True)
            o = (jnp.dot(p_nb.astype(BF16), vs[:, sl], preferred_element_type=F32)
                 + jnp.dot(p_cx.astype(BF16), vc[:, sl], preferred_element_type=F32))
            outs.append(o / l)
        o_ref[...] = jnp.concatenate(outs, axis=-1)

    @pl.when(i >= rows)
    def _():
        outs = []
        for h in range(N_HEADS):
            sl = slice(HEAD_DIM * h, HEAD_DIM * (h + 1))
            s_cx = lax.dot_general(q[:, sl], kc[:, sl], nt, preferred_element_type=F32)
            m = jnp.max(s_cx, axis=-1, keepdims=True)
            p_cx = jnp.exp(s_cx - m)
            l = jnp.sum(p_cx, axis=-1, keepdims=True)
            o = jnp.dot(p_cx.astype(BF16), vc[:, sl], preferred_element_type=F32)
            outs.append(o / l)
        o_ref[...] = jnp.concatenate(outs, axis=-1)


def _neighbourhood(q, k, v, tab, S):
    T = q.shape[0]
    rows = S // GRID_W
    n_ctx = T - S
    resident = lambda shape: pl.BlockSpec(shape, lambda i: (0,) * len(shape),
                                          pipeline_mode=pl.Buffered(1))
    return pl.pallas_call(
        functools.partial(_na_kernel, rows=rows, n_ctx=n_ctx),
        grid=(T // GRID_W,),
        in_specs=[pl.BlockSpec((GRID_W, 256), lambda i: (i, 0)),
                  resident((T, 256)), resident((T, 256)),
                  resident((N_HEADS, NA_KH, GRID_W, NA_KH * GRID_W))],
        out_specs=pl.BlockSpec((GRID_W, 256), lambda i: (i, 0)),
        out_shape=jax.ShapeDtypeStruct((T, 256), F32),
        compiler_params=_params(1), name="nbr_attn",
    )(q, k, v, tab)


def _lru_kernel(x_ref, xp_ref, xn_ref, cw_ref, cb_ref, wa_ref, ba_ref, wi_ref, bi_ref,
                lam_ref, y_ref, h_scr, *, reverse, n_lat, n_ctx):
    j = pl.program_id(0)
    tc = x_ref.shape[0]
    n_all = n_lat + n_ctx
    if reverse:
        chunk = jnp.where(j < n_ctx, n_all - 1 - j, n_all - 1 - j)
    else:
        chunk = jnp.where(j < n_ctx, n_lat + j, j - n_ctx)
    has_prev = jnp.logical_and(chunk != 0, chunk != n_lat)
    has_next = jnp.logical_and(chunk != n_lat - 1, chunk != n_all - 1)

    @pl.when(j == 0)
    def _():
        h_scr[...] = jnp.zeros(h_scr.shape, F32)

    xp = jnp.where(has_prev, xp_ref[...], 0.0)
    xn = jnp.where(has_next, xn_ref[...], 0.0)
    xe = jnp.concatenate([xp, x_ref[...], xn], axis=0)
    ne = tc + 16
    cw = cw_ref[...]
    xc = (cw[0:1] * pltpu.roll(xe, 1, 0)[8:8 + tc]
          + cw[1:2] * xe[8:8 + tc]
          + cw[2:3] * pltpu.roll(xe, ne - 1, 0)[8:8 + tc]
          + cw[3:4] * pltpu.roll(xe, ne - 2, 0)[8:8 + tc]
          + cb_ref[...])

    r = jax.nn.sigmoid(jnp.dot(xc, wa_ref[0], preferred_element_type=F32, precision=HIGHEST)
                       + ba_ref[0])
    gi = jax.nn.sigmoid(jnp.dot(xc, wi_ref[0], preferred_element_type=F32, precision=HIGHEST)
                        + bi_ref[0])
    z = -lam_ref[0]
    softplus = jnp.maximum(z, 0.0) + jnp.log1p(jnp.exp(-jnp.abs(z)))
    log_a = -LRU_C * r * softplus
    a = jnp.exp(log_a)
    b = jnp.sqrt(-jnp.tanh(log_a) * (a * a + 1.0)) * (gi * xc)

    rowi = lax.broadcasted_iota(jnp.int32, (tc, 1), 0)
    s = 1
    while s < tc:
        if reverse:
            keep = rowi < tc - s
            a_s = jnp.where(keep, pltpu.roll(a, tc - s, 0), 1.0)
            b_s = jnp.where(keep, pltpu.roll(b, tc - s, 0), 0.0)
        else:
            keep = rowi >= s
            a_s = jnp.where(keep, pltpu.roll(a, s, 0), 1.0)
            b_s = jnp.where(keep, pltpu.roll(b, s, 0), 0.0)
        b = a * b_s + b
        a = a * a_s
        s *= 2
    hcur = b + a * h_scr[...]
    y_ref[...] = hcur
    h_scr[...] = hcur[0:1] if reverse else hcur[tc - 1:tc]


def _lru_scan(lx, lw, S, tc, reverse):
    T = lx.shape[0]
    n_lat, n_ctx = S // tc, (T - S) // tc
    n_all = n_lat + n_ctx
    d = 1 if reverse else 0
    t8 = tc // 8

    def chunk_of(j):
        if reverse:
            return n_all - 1 - j
        return jnp.where(j < n_ctx, n_lat + j, j - n_ctx)

    dspec = lambda shape: pl.BlockSpec((1,) + shape, lambda j: (d,) + (0,) * len(shape))
    return pl.pallas_call(
        functools.partial(_lru_kernel, reverse=reverse, n_lat=n_lat, n_ctx=n_ctx),
        grid=(n_all,),
        in_specs=[pl.BlockSpec((tc, 256), lambda j: (chunk_of(j), 0)),
                  pl.BlockSpec((8, 256), lambda j: (jnp.maximum(chunk_of(j) * t8 - 1, 0), 0)),
                  pl.BlockSpec((8, 256),
                               lambda j: (jnp.minimum((chunk_of(j) + 1) * t8, T // 8 - 1), 0)),
                  _full((4, 256)), _full((1, 256)),
                  dspec((256, 256)), dspec((1, 256)), dspec((256, 256)), dspec((1, 256)),
                  dspec((1, 256))],
        out_specs=pl.BlockSpec((tc, 256), lambda j: (chunk_of(j), 0)),
        out_shape=jax.ShapeDtypeStruct((T, 256), F32),
        scratch_shapes=[pltpu.VMEM((1, 256), F32)],
        compiler_params=_params(1), name="rglru_bwd" if reverse else "rglru_fwd",
    )(lx, lx, lx, lw["conv_w"], lw["conv_b"], lw["wa"], lw["ba"], lw["wi"], lw["bi"], lw["lam"])


def _merge_kernel(x_ref, mod_ref, oa_ref, obl_ref, obc_ref, ocl_ref, occ_ref, yf_ref, yb_ref,
                  lg_ref, gg_ref, wout_ref, g2_ref, rw_ref, rb_ref, lt_ref, ut_ref,
                  x1_o, tok_o, slot_o, gate_o, cnt_o, *, n_lat_tiles):
    i = pl.program_id(0)
    tm = x_ref.shape[0]
    is_ctx = i >= n_lat_tiles

    gg = gg_ref[...]
    od = (yf_ref[...] + yb_ref[...]) * lg_ref[...]
    ob = jnp.where(is_ctx, obc_ref[...], obl_ref[...])
    oc = jnp.where(is_ctx, occ_ref[...], ocl_ref[...])
    parts = [oa_ref[...], ob, oc, od]
    ycat = jnp.concatenate(
        [(_rms(p) * gg[:, GROUP_W * n:GROUP_W * (n + 1)]).astype(BF16) for n, p in enumerate(parts)],
        axis=-1)
    y = jnp.dot(ycat, wout_ref[...], preferred_element_type=F32)
    gt1 = mod_ref[0, 2:3, :]
    sh2 = mod_ref[0, 3:4, :]
    sc2 = mod_ref[0, 4:5, :]
    x1 = x_ref[...] + gt1 * y
    x1_o[...] = x1
    tok = _rms(x1) * g2_ref[...] * (1.0 + sc2) + sh2
    tok_o[...] = tok.astype(BF16)

    logits = jnp.dot(tok, rw_ref[...], preferred_element_type=F32, precision=HIGHEST) + rb_ref[...]
    lane = lax.broadcasted_iota(jnp.int32, (tm, LANES), 1)
    v_acc = jnp.zeros((tm, LANES), F32)
    onehots = []
    v0 = None
    work = logits
    for k in range(TOP_K):
        vk = jnp.max(work, axis=-1, keepdims=True)
        ek = jnp.min(jnp.where(work == vk, lane, LANES), axis=-1, keepdims=True)
        sel = lane == ek
        onehots.append(sel)
        work = jnp.where(sel, NEG * 2.0, work)
        if k == 0:
            v0 = vk
        v_acc = jnp.where(lane == k, jnp.exp(vk - v0), v_acc)
    gate_o[...] = v_acc / jnp.sum(v_acc, axis=-1, keepdims=True)

    oh = [jnp.where(o, 1.0, 0.0) for o in onehots]
    oh_all = oh[0] + oh[1] + oh[2] + oh[3]
    cnt = jnp.sum(oh_all, axis=0, keepdims=True)
    run = jnp.floor((cnt + (RUN_ALIGN - 1.0)) * (1.0 / RUN_ALIGN)) * RUN_ALIGN
    run_start = jnp.dot(jnp.broadcast_to(run, (8, LANES)).astype(BF16), ut_ref[...],
                        preferred_element_type=F32)[0:1]
    excl = jnp.dot(lt_ref[...], oh_all.astype(BF16), preferred_element_type=F32) + run_start
    slot = jnp.full((tm, LANES), -1.0, F32)
    for k in range(TOP_K):
        sk = jnp.sum(oh[k] * excl, axis=-1, keepdims=True)
        slot = jnp.where(lane == k, sk, slot)
    slot_o[...] = slot.astype(jnp.int32)
    cnt_o[0] = jnp.broadcast_to(cnt, (8, LANES))


def _merge(x, mod, oa, ob_lat, ob_ctx, oc_lat, oc_ctx, yf, yb, lg, S, tm, consts, lw):
    T, D = x.shape
    nL = S // tm
    nt = T // tm
    row = lambda w: pl.BlockSpec((tm, w), lambda i: (i, 0))
    lat = pl.BlockSpec((tm, 256), lambda i: (jnp.minimum(i, nL - 1), 0))
    ctx = pl.BlockSpec((tm, 256), lambda i: (jnp.maximum(i - nL, 0), 0))
    sds = jax.ShapeDtypeStruct
    return pl.pallas_call(
        functools.partial(_merge_kernel, n_lat_tiles=nL), grid=(nt,),
        in_specs=[row(D), pl.BlockSpec((1, 6, D), lambda i: (jnp.where(i >= nL, 1, 0), 0, 0)),
                  row(256), lat, ctx, lat, ctx, row(256), row(256), row(256),
                  _full((1, D)), _full((D, D)), _full((1, D)), _full((D, LANES)),
                  _full((1, LANES)), _full((tm, tm)), _full((LANES, LANES))],
        out_specs=[row(D), row(D), row(LANES), row(LANES),
                   pl.BlockSpec((1, 8, LANES), lambda i: (i, 0, 0))],
        out_shape=[sds((T, D), F32), sds((T, D), BF16), sds((T, LANES), jnp.int32),
                   sds((T, LANES), F32), sds((nt, 8, LANES), F32)],
        compiler_params=_params(1), name="merge_router",
    )(x, mod, oa, ob_lat, ob_ctx, oc_lat, oc_ctx, yf, yb, lg, lw["grp_g"], lw["w_out"], lw["g2"],
      lw["router_w"], lw["router_b"], consts["lt"], consts["ut"])


def _routing_meta(cnt, bm, n_rows):
    c = cnt[:, 0, :N_EXPERTS].astype(jnp.int32)
    nt = c.shape[0]
    run = (c + RUN_ALIGN - 1) // RUN_ALIGN * RUN_ALIGN
    tile_off = jnp.cumsum(run, axis=0) - run
    tot = jnp.sum(run, axis=0)
    padded = (tot + bm - 1) // bm * bm
    seg_end = jnp.cumsum(padded)
    seg_start = seg_end - padded
    g_start = seg_start[None, :] + tile_off
    l_start = jnp.cumsum(run, axis=1) - run
    per_size, done = [], jnp.zeros_like(run)
    for size in CHUNK_ROWS:
        n = (run - done) // size
        per_size.append(jnp.sum(n, axis=1, keepdims=True))
        done = done + n * size
    n_tot = jnp.sum(run, axis=1, keepdims=True) // RUN_ALIGN
    meta = jnp.concatenate(
        [g_start, l_start, run] + per_size
        + [n_tot, jnp.zeros((nt, LANES - 3 * N_EXPERTS - len(CHUNK_ROWS) - 1), jnp.int32)], axis=1)
    n_zero = (padded - tot) // RUN_ALIGN
    zmeta = jnp.concatenate([seg_start + tot, n_zero, jnp.sum(n_zero)[None], seg_end[-1:] // bm,
                             jnp.zeros((LANES - 2 * N_EXPERTS - 2,), jnp.int32)])
    blk_row = jnp.arange(n_rows // bm, dtype=jnp.int32) * bm
    blk_expert = jnp.minimum(jnp.sum((seg_end[None, :] <= blk_row[:, None]).astype(jnp.int32), axis=1),
                             N_EXPERTS - 1)
    n_used = seg_end[-1:] // bm
    return (meta.reshape(nt, 1, LANES).astype(jnp.int32), zmeta.reshape(1, 1, LANES).astype(jnp.int32),
            blk_expert.astype(jnp.int32), n_used.astype(jnp.int32))


def _run_copies(meta_ref, make_copy):
    def per_expert(e, c):
        g = meta_ref[0, 0, e]
        l = meta_ref[0, 0, N_EXPERTS + e]
        rows = meta_ref[0, 0, 2 * N_EXPERTS + e]
        done = 0
        for size in CHUNK_ROWS:
            n = (rows - done) // size
            start = done

            def issue(cc, c2, size=size, start=start):
                off = start + size * cc
                make_copy(pl.multiple_of(g + off, RUN_ALIGN), pl.multiple_of(l + off, RUN_ALIGN),
                          size).start()
                return c2

            c = lax.fori_loop(0, n, issue, c)
            done = done + n * size
        return c

    lax.fori_loop(0, N_EXPERTS, per_expert, 0)


def _drain(meta_ref, make_copy):
    for n, size in enumerate(CHUNK_ROWS):
        def body(_, c, size=size):
            make_copy(0, 0, size).wait()
            return c

        lax.fori_loop(0, meta_ref[0, 0, 3 * N_EXPERTS + n], body, 0)


def _dispatch_kernel(meta_ref, zmeta_ref, slot_ref, tok_ref, xin_ref, xs_scr, z_scr, sem):
    i = pl.program_id(0)
    tm = tok_ref.shape[0]
    n_slot = xs_scr.shape[0]
    slot_t = jnp.transpose(slot_ref[...].astype(F32))
    srow = lax.broadcasted_iota(jnp.int32, (n_slot, tm), 0).astype(F32)
    perm = jnp.zeros((n_slot, tm), F32)
    for k in range(TOP_K):
        perm = perm + jnp.where(srow == slot_t[k:k + 1, :], 1.0, 0.0)
    xs_scr[...] = jnp.dot(perm.astype(BF16), tok_ref[...], preferred_element_type=F32)

    def to_global(g_row, l_row, size):
        return pltpu.make_async_copy(xs_scr.at[pl.ds(l_row, size)],
                                     xin_ref.at[pl.ds(g_row, size)], sem)

    _run_copies(meta_ref, to_global)
    _drain(meta_ref, to_global)

    @pl.when(i == pl.num_programs(0) - 1)
    def _():
        z_scr[...] = jnp.zeros(z_scr.shape, F32)

        def zero_copy(g_row):
            return pltpu.make_async_copy(z_scr.at[pl.ds(0, RUN_ALIGN)],
                                         xin_ref.at[pl.ds(g_row, RUN_ALIGN)], sem)

        def per_expert(e, c):
            g = zmeta_ref[0, 0, e]

            def issue(cc, c2):
                zero_copy(pl.multiple_of(g + RUN_ALIGN * cc, RUN_ALIGN)).start()
                return c2

            return lax.fori_loop(0, zmeta_ref[0, 0, N_EXPERTS + e], issue, c)

        lax.fori_loop(0, N_EXPERTS, per_expert, 0)

        def drain(_, c):
            zero_copy(0).wait()
            return c

        lax.fori_loop(0, zmeta_ref[0, 0, 2 * N_EXPERTS], drain, 0)

        bm = z_scr.shape[0]

        def block_copy(b):
            return pltpu.make_async_copy(z_scr, xin_ref.at[pl.ds(pl.multiple_of(b * bm, bm), bm)], sem)

        def issue_block(b, c):
            block_copy(b).start()
            return c

        def drain_block(b, c):
            block_copy(b).wait()
            return c

        n_used = zmeta_ref[0, 0, 2 * N_EXPERTS + 1]
        lax.fori_loop(n_used, xin_ref.shape[0] // bm, issue_block, 0)
        lax.fori_loop(n_used, xin_ref.shape[0] // bm, drain_block, 0)


def _dispatch(tok, slot, meta, zmeta, n_rows, tm, bm):
    T, D = tok.shape
    n_slot = TOP_K * tm + N_EXPERTS * RUN_ALIGN
    smem = lambda idx: pl.BlockSpec((1, 1, LANES), idx, memory_space=pltpu.SMEM)
    return pl.pallas_call(
        _dispatch_kernel,
        grid=(T // tm,),
        in_specs=[smem(lambda i: (i, 0, 0)), smem(lambda i: (0, 0, 0)),
                  pl.BlockSpec((tm, LANES), lambda i: (i, 0)),
                  pl.BlockSpec((tm, D), lambda i: (i, 0))],
        out_specs=pl.BlockSpec(memory_space=pl.ANY),
        out_shape=jax.ShapeDtypeStruct((n_rows, D), F32),
        scratch_shapes=[pltpu.VMEM((n_slot, D), F32), pltpu.VMEM((bm, D), F32),
                        pltpu.SemaphoreType.DMA(())],
        compiler_params=_params(1), name="moe_dispatch",
    )(meta, zmeta, slot, tok)


def _expert_kernel(be_ref, nu_ref, x_ref, wgu_ref, bgu_ref, wd_ref, bd_ref, y_ref):
    i = pl.program_id(0)
    d_e = wd_ref.shape[1]

    @pl.when(i < nu_ref[0])
    def _():
        gu = jnp.dot(x_ref[...].astype(BF16), wgu_ref[0], preferred_element_type=F32) + bgu_ref[0]
        x_glu = jnp.minimum(gu[:, :d_e], SWIGLU_LIMIT)
        x_lin = jnp.clip(gu[:, d_e:], -SWIGLU_LIMIT, SWIGLU_LIMIT)
        act = x_glu * jax.nn.sigmoid(SWIGLU_ALPHA * x_glu) * (x_lin + 1.0)
        y_ref[...] = jnp.dot(act.astype(BF16), wd_ref[0], preferred_element_type=F32) + bd_ref[0]

    @pl.when(i >= nu_ref[0])
    def _():
        y_ref[...] = jnp.zeros(y_ref.shape, F32)


def _experts(xin, blk_expert, n_used, layer, w_gu, b_gu, w_down, b_down, bm):
    P, D = xin.shape
    d2 = w_gu.shape[-1]
    d_e = d2 // 2
    blk = lambda i, be, nu: jnp.minimum(i, nu[0] - 1)
    wspec = lambda r, c: pl.BlockSpec((None, 1, r, c),
                                      lambda i, be, nu: (layer, be[blk(i, be, nu)], 0, 0))
    grid_spec = pltpu.PrefetchScalarGridSpec(
        num_scalar_prefetch=2, grid=(P // bm,),
        in_specs=[pl.BlockSpec((bm, D), lambda i, be, nu: (blk(i, be, nu), 0)),
                  wspec(D, d2), wspec(1, d2), wspec(d_e, D), wspec(1, D)],
        out_specs=pl.BlockSpec((bm, D), lambda i, be, nu: (i, 0)))
    return pl.pallas_call(
        _expert_kernel, grid_spec=grid_spec,
        out_shape=jax.ShapeDtypeStruct((P, D), F32),
        compiler_params=_params(1), name="moe_experts",
    )(blk_expert, n_used, xin, w_gu, b_gu, w_down, b_down)


def _combine_kernel(meta_ref, slot_ref, gate_ref, y_ref, x_ref, mod_ref, o_ref, ybuf, sem):
    tm = x_ref.shape[0]
    n_slot, d = ybuf.shape

    def to_local(g_row, l_row, size):
        return pltpu.make_async_copy(y_ref.at[pl.ds(g_row, size)],
                                     ybuf.at[pl.ds(l_row, size)], sem)

    _run_copies(meta_ref, to_local)
    n_tot = meta_ref[0, 0, 3 * N_EXPERTS + len(CHUNK_ROWS)]

    def zero_tail(cc, c):
        ybuf[pl.ds(pl.multiple_of(RUN_ALIGN * cc, RUN_ALIGN), RUN_ALIGN), :] = jnp.zeros(
            (RUN_ALIGN, d), F32)
        return c

    lax.fori_loop(n_tot, n_slot // RUN_ALIGN, zero_tail, 0)
    _drain(meta_ref, to_local)

    slot = slot_ref[...]
    gate = gate_ref[...]
    col = lax.broadcasted_iota(jnp.int32, (tm, n_slot), 1)
    w = jnp.zeros((tm, n_slot), F32)
    for k in range(TOP_K):
        w = w + jnp.where(col == slot[:, k:k + 1], gate[:, k:k + 1], 0.0)
    w_hi = w.astype(BF16)
    w_lo = (w - w_hi.astype(F32)).astype(BF16)
    yb = ybuf[...].astype(BF16)
    f = (jnp.dot(w_hi, yb, preferred_element_type=F32)
         + jnp.dot(w_lo, yb, preferred_element_type=F32))
    o_ref[...] = x_ref[...] + mod_ref[0, 5:6, :] * f


def _combine(y, slot, gate, meta, x1, mod, S, tm):
    T, D = x1.shape
    nL = S // tm
    n_slot = TOP_K * tm + N_EXPERTS * RUN_ALIGN
    return pl.pallas_call(
        _combine_kernel,
        grid=(T // tm,),
        in_specs=[pl.BlockSpec((1, 1, LANES), lambda i: (i, 0, 0), memory_space=pltpu.SMEM),
                  pl.BlockSpec((tm, LANES), lambda i: (i, 0)),
                  pl.BlockSpec((tm, LANES), lambda i: (i, 0)),
                  pl.BlockSpec(memory_space=pl.ANY),
                  pl.BlockSpec((tm, D), lambda i: (i, 0)),
                  pl.BlockSpec((1, 6, D), lambda i: (jnp.where(i >= nL, 1, 0), 0, 0))],
        out_specs=pl.BlockSpec((tm, D), lambda i: (i, 0)),
        out_shape=jax.ShapeDtypeStruct((T, D), F32),
        scratch_shapes=[pltpu.VMEM((n_slot, D), F32), pltpu.SemaphoreType.DMA(())],
        compiler_params=_params(1), name="moe_combine",
    )(meta, slot, gate, y, x1, mod)


def _rope_tables(S, T, tm, rot_dim, width, lane0, reps):
    h = rot_dim // 4
    ax = rot_dim // 2
    inv = ROPE_THETA ** (-jnp.arange(0, ax, 2, dtype=F32) / ax)

    def parts(n):
        ang = jnp.arange(n, dtype=F32)[:, None] * inv
        return jnp.cos(ang), jnp.sin(ang), jnp.zeros((n, h), F32)

    def place(blocks, fill):
        n = blocks[0].shape[0]
        tab = jnp.concatenate(
            [jnp.full((n, lane0), fill, F32)] + list(blocks)
            + [jnp.full((n, width - lane0 - rot_dim), fill, F32)], axis=-1)
        return jnp.tile(tab, (1, reps))

    n_rows = S // GRID_W
    c, s, z = parts(n_rows)
    row = jnp.concatenate([place([c, c, z, z], 0.0), place([z, s, z, z], 0.0),
                           place([-s, z, z, z], 0.0)], axis=-1)
    row = jnp.concatenate([row, jnp.zeros(((T - S) // GRID_W, row.shape[1]), F32)], axis=0)
    c, s, z = parts(GRID_W)
    col = jnp.concatenate([place([z, z, c, c], 1.0), place([z, z, z, s], 0.0),
                           place([z, z, -s, z], 0.0)], axis=-1)
    col = jnp.tile(col, (tm // GRID_W, 1))
    ident = jnp.concatenate([jnp.ones((tm, reps * width), F32),
                             jnp.zeros((tm, 2 * reps * width), F32)], axis=-1)
    return row.reshape(T // tm, tm // GRID_W, -1), jnp.stack([col, ident])


def _constants(S, T, tm):
    rope_q_row, rope_q_col = _rope_tables(S, T, tm, HEAD_DIM, HEAD_DIM, 0, N_HEADS)
    rope_m_row, rope_m_col = _rope_tables(S, T, tm, MLA_ROPE, MLA_PAD, MLA_NOPE, 1)
    bd64 = np.kron(np.eye(4, dtype=np.float32), np.ones((64, 64), np.float32))
    sizes = [MLA_NOPE, MLA_ROPE, MLA_PAD - MLA_NOPE - MLA_ROPE] * N_HEADS
    gid = np.repeat(np.arange(len(sizes)), sizes)
    gm = (gid[:, None] == gid[None, :]).astype(np.float32)
    invn = (1.0 / np.repeat(np.asarray(sizes, np.float32), sizes))[None, :]
    lt = np.tril(np.ones((tm, tm), np.float32), -1)
    ut = np.triu(np.ones((LANES, LANES), np.float32), 1)
    return dict(rope_q_row=rope_q_row, rope_q_col=rope_q_col,
                rope_m_row=rope_m_row, rope_m_col=rope_m_col,
                bd64=jnp.asarray(bd64, BF16), gm=jnp.asarray(gm, BF16),
                invn=jnp.asarray(invn), lt=jnp.asarray(lt, BF16), ut=jnp.asarray(ut, BF16))


def _na_bias_tables(rpb):
    qc = np.arange(GRID_W)[:, None]
    kc = np.arange(GRID_W)[None, :]
    c0 = np.clip(qc - NA_KW // 2, 0, GRID_W - NA_KW)
    valid = (kc >= c0) & (kc < c0 + NA_KW)
    sel = ((kc - qc + NA_KW - 1)[:, :, None] == np.arange(2 * NA_KW - 1)) & valid[:, :, None]
    toep = jnp.einsum("lhab,qkb->lhaqk", rpb, jnp.asarray(sel, F32), precision=HIGHEST)
    toep = jnp.where(jnp.asarray(valid)[None, None, None], toep, NEG)
    L, H = rpb.shape[:2]
    tabs = []
    for off in range(NA_KH):
        rows = toep[:, :, NA_KH - 1 - off:2 * NA_KH - 1 - off]
        tabs.append(rows.transpose(0, 1, 3, 2, 4).reshape(L, H, GRID_W, NA_KH * GRID_W))
    return jnp.stack(tabs, axis=2)


def _block_diag(w):
    n, bw, _ = w.shape
    eye = jnp.eye(n, dtype=w.dtype)
    return (eye[:, None, :, None] * w[:, :, None, :]).reshape(n * bw, n * bw)


def _layer_weights(l, p):
    D = p["w_in"].shape[1]
    w_in = p["w_in"][l]
    offs = np.cumsum([0, 256, 256, 256, 256, 128, 128, 256, 128, 32, 256, 256])
    seg = lambda n: w_in[:, offs[n]:offs[n + 1]]
    z = lambda n: jnp.zeros((D, n), F32)
    w_in_r = jnp.concatenate(
        [seg(0), seg(1), seg(2), seg(3), seg(4), seg(5), seg(6), seg(7), seg(9), seg(10),
         z(MLA_NOPE), seg(8), z(MLA_PAD - MLA_NOPE - MLA_ROPE)], axis=-1).astype(BF16)
    sc = HEAD_DIM ** -0.5
    sc_m = (MLA_NOPE + MLA_ROPE) ** -0.5
    t4 = lambda g: jnp.tile(g, N_HEADS)[None, :]
    wuq = p["mla_wuq"][l].reshape(-1, N_HEADS, MLA_NOPE + MLA_ROPE)
    wuq = jnp.pad(wuq, ((0, 0), (0, 0), (0, MLA_PAD - MLA_NOPE - MLA_ROPE)))
    wukv = p["mla_wukv"][l].reshape(-1, N_HEADS, MLA_NOPE + HEAD_DIM)
    wukv_k = jnp.pad(wukv[:, :, :MLA_NOPE], ((0, 0), (0, 0), (0, MLA_PAD - MLA_NOPE)))
    qn, kn = p["mla_qn"][l], p["mla_kn"][l]
    padq = jnp.pad(qn * (sc_m * LOG2E), (0, MLA_PAD - MLA_NOPE - MLA_ROPE))
    padkn = jnp.pad(kn[:MLA_NOPE], (0, MLA_PAD - MLA_NOPE))
    padkr = jnp.pad(kn[MLA_NOPE:], (MLA_NOPE, MLA_PAD - MLA_NOPE - MLA_ROPE))
    rw =jnp.pad(p["router_w"][l], ((0, 0), (0, LANES - N_EXPERTS)))
    rb = jnp.pad(p["router_b"][l], (0, LANES - N_EXPERTS), constant_values=NEG)[None, :]
    return dict(
        g1=p["norm1_g"][l][None, :], g2=p["norm2_g"][l][None, :], w_in=w_in_r,
        na_qn=t4(p["na_qn"][l] * sc), na_kn=t4(p["na_kn"][l]),
        gqa_qn=t4(p["gqa_qn"][l] * (sc * LOG2E)), gqa_kn=jnp.tile(p["gqa_kn"][l], GQA_KV_HEADS)[None, :],
        qa_g=p["mla_qa_g"][l][None, :], kva_g=p["mla_kva_g"][l][None, :],
        mla_qn=t4(padq), mla_knn=t4(padkn), mla_knr=padkr[None, :],
        wuq=wuq.reshape(-1, N_HEADS * MLA_PAD).astype(BF16),
        wukv_k=wukv_k.reshape(-1, N_HEADS * MLA_PAD).astype(BF16),
        wukv_v=wukv[:, :, MLA_NOPE:].reshape(-1, N_HEADS * HEAD_DIM).astype(BF16),
        conv_w=p["lru_conv_w"][l][:, 0, :], conv_b=p["lru_conv_b"][l][None, :],
        wa=jnp.stack([_block_diag(p["lru_wa"][l][d]) for d in range(2)]),
        wi=jnp.stack([_block_diag(p["lru_wi"][l][d]) for d in range(2)]),
        ba=p["lru_ba"][l][:, None, :], bi=p["lru_bi"][l][:, None, :],
        lam=p["lru_lam"][l][:, None, :],
        grp_g=p["grp_g"][l][None, :], w_out=p["w_out"][l].astype(BF16),
        router_w=rw, router_b=rb,
    )


def _tiles(S, C):
    tm = min(256, C)
    tq = min(1024, S)
    T = S + C
    tk = next(t for t in (1280, 640, 256, 128) if T % t == 0)
    return dict(tm=tm, tq=tq, tk=tk, tc=min(256, C), bm=256, tt=min(128, C))


def kernel(x, c, ctx, c_ctx, ada_w, ada_b, norm1_g, norm2_g, w_in, na_qn, na_kn, na_rpb, gqa_qn, gqa_kn, mla_qa_g, mla_kva_g, mla_wuq, mla_wukv, mla_qn, mla_kn, lru_conv_w, lru_conv_b, lru_wa, lru_ba, lru_wi, lru_bi, lru_lam, grp_g, w_out, router_w, router_b, exp_w_gu, exp_b_gu, exp_w_down, exp_b_down):
    p = dict(norm1_g=norm1_g, norm2_g=norm2_g, w_in=w_in, na_qn=na_qn, na_kn=na_kn, na_rpb=na_rpb,
             gqa_qn=gqa_qn, gqa_kn=gqa_kn, mla_qa_g=mla_qa_g, mla_kva_g=mla_kva_g,
             mla_wuq=mla_wuq, mla_wukv=mla_wukv, mla_qn=mla_qn, mla_kn=mla_kn,
             lru_conv_w=lru_conv_w, lru_conv_b=lru_conv_b, lru_wa=lru_wa, lru_ba=lru_ba,
             lru_wi=lru_wi, lru_bi=lru_bi, lru_lam=lru_lam, grp_g=grp_g, w_out=w_out,
             router_w=router_w, router_b=router_b, exp_w_gu=exp_w_gu, exp_b_gu=exp_b_gu,
             exp_w_down=exp_w_down, exp_b_down=exp_b_down)
    B, S, D = x.shape
    assert B == 1 and S % GRID_W == 0
    C = ctx.shape[1]
    T = S + C
    L = ada_w.shape[0]
    ts = _tiles(S, C)
    tm, bm = ts["tm"], ts["bm"]
    consts = _constants(S, T, tm)

    c8 = jnp.zeros((8, D), F32).at[0].set(c[0]).at[1].set(c_ctx)
    mods = _modulation(c8, ada_w, ada_b)[:, :2].reshape(L, 2, 6, D)

    n_rows = -(-(T * TOP_K + (T // tm) * N_EXPERTS * (RUN_ALIGN - 1) + N_EXPERTS * (bm - 1)) // bm) * bm
    na_tabs = _na_bias_tables(na_rpb)
    n_exp = exp_w_gu.shape[1]
    w_gu, w_down = exp_w_gu.astype(BF16), exp_w_down.astype(BF16)
    b_gu, b_down = exp_b_gu.reshape(L, n_exp, 1, -1), exp_b_down.reshape(L, n_exp, 1, -1)
    tq, tk = ts["tq"], ts["tk"]
    xs = jnp.concatenate([x[0], ctx[0]], axis=0)
    for l in range(L):
        lw = _layer_weights(l, p)
        mod = mods[l]
        (naq, nak, nav, gq, gk, gv, mq, mk, mv, lx, lg) = _premix(xs, mod, S, tm, consts, lw)
        oa = _neighbourhood(naq, nak, nav, na_tabs[l], S)
        ob_lat = _flash(gq, gk, gv, tq, tk, S // tq, 0, T // tk, 0)
        ob_ctx = _flash(gq, gk, gv, C, C, 1, S // C, 1, S // C)
        oc_lat = _flash(mq, mk, mv, tq, tk, S // tq, 0, T // tk, 0)
        oc_ctx = _flash(mq, mk, mv, C, C, 1, S // C, 1, S // C)
        yf = _lru_scan(lx, lw, S, ts["tc"], False)
        yb = _lru_scan(lx, lw, S, ts["tc"], True)
        x1, tok, slot, gate, cnt = _merge(xs, mod, oa, ob_lat, ob_ctx, oc_lat, oc_ctx, yf, yb, lg,
                                          S, tm, consts, lw)
        meta, zmeta, blk_expert, n_used = _routing_meta(cnt, bm, n_rows)
        xin = _dispatch(tok, slot, meta, zmeta, n_rows, tm, bm)
        y = _experts(xin, blk_expert, n_used, l, w_gu, b_gu, w_down, b_down, bm)
        xs = _combine(y, slot, gate, meta, x1, mod, S, tm)
    return xs[:S][None]
```

```python
import functools

import numpy as np
import jax
import jax.numpy as jnp
from jax import lax
from jax.experimental import pallas as pl
from jax.experimental.pallas import tpu as pltpu

F32 = jnp.float32
BF16 = jnp.bfloat16
HIGHEST = lax.Precision.HIGHEST

GRID_W = 64
HEAD_DIM = 64
N_HEADS = 4
GROUP_W = 256
GQA_KV_HEADS = 2
NA_KH = 8
NA_KW = 16
MLA_NOPE = 64
MLA_ROPE = 32
MLA_PAD = 128
LRU_C = 8.0
N_EXPERTS = 32
TOP_K = 4
SWIGLU_LIMIT = 7.0
SWIGLU_ALPHA = 1.702
ROPE_THETA = 10000.0
EPS = 1e-6
NEG = -1e30
LANES = 128
FLASH_ROW_BLOCK = 256
LOG2E = 1.4426950408889634
RUN_ALIGN = 8
CHUNK_ROWS = (32, 16, 8)
VMEM_LIMIT = 56 * 1024 * 1024

_C_NAQ, _C_NAK, _C_NAV = 0, 256, 512
_C_GQ, _C_GK, _C_GV = 768, 1024, 1152
_C_MCQ, _C_MCKV = 1280, 1536
_C_LX, _C_LG = 1664, 1920
_C_MKR = 2176
IN_COLS_R = 2304


def _params(n_axes, vmem=VMEM_LIMIT):
    return pltpu.CompilerParams(dimension_semantics=("arbitrary",) * n_axes,
                                vmem_limit_bytes=vmem)


def _full(shape):
    n = len(shape)
    return pl.BlockSpec(shape, lambda *_: (0,) * n)


def _mod_kernel(c_ref, w_ref, b_ref, o_ref):
    cc = c_ref[...]
    s = cc * jax.nn.sigmoid(cc)
    o_ref[0] = jnp.dot(s, w_ref[0], preferred_element_type=F32, precision=HIGHEST) + b_ref[0]


def _modulation(c8, ada_w, ada_b):
    L, D, D6 = ada_w.shape
    tn = D6 // 6
    return pl.pallas_call(
        _mod_kernel,
        grid=(L, D6 // tn),
        in_specs=[pl.BlockSpec((8, D), lambda l, j: (0, 0)),
                  pl.BlockSpec((1, D, tn), lambda l, j: (l, 0, j)),
                  pl.BlockSpec((1, 1, tn), lambda l, j: (l, 0, j))],
        out_specs=pl.BlockSpec((1, 8, tn), lambda l, j: (l, 0, j)),
        out_shape=jax.ShapeDtypeStruct((L, 8, D6), F32),
        compiler_params=_params(2),
        name="adaln_mod",
    )(c8, ada_w, ada_b.reshape(L, 1, D6))


def _rms(z):
    return z * lax.rsqrt(jnp.mean(z * z, axis=-1, keepdims=True) + EPS)


def _group_rms(z, gmat, inv_n):
    zz = z * z
    hi = zz.astype(BF16)
    lo = (zz - hi.astype(F32)).astype(BF16)
    ss = (jnp.dot(hi, gmat, preferred_element_type=F32)
          + jnp.dot(lo, gmat, preferred_element_type=F32))
    return z * lax.rsqrt(ss * inv_n + EPS)


def _rope(z, cos, sin_lo, sin_hi, half):
    w = z.shape[-1]
    return z * cos + pltpu.roll(z, half, 1) * sin_lo + pltpu.roll(z, w - half, 1) * sin_hi


def _tile4(t):
    return jnp.concatenate([t, t, t, t], axis=-1)


def _rope_tile(row_ref, col_ref, width):
    rt = row_ref[0]
    rows = jnp.concatenate([jnp.broadcast_to(rt[g:g + 1], (GRID_W, rt.shape[1]))
                            for g in range(rt.shape[0])], axis=0)
    tab = rows + col_ref[0]
    return tab[:, :width], tab[:, width:2 * width], tab[:, 2 * width:]


def _premix_kernel(x_ref, mod_ref, g1_ref, win_ref, bd_ref, gm_ref,
                   rq_ref, cq_ref, rm_ref, cm_ref,
                   naqn_ref, nakn_ref, gqn_ref, gkn_ref, qag_ref, kvag_ref,
                   mqn_ref, mknn_ref, mknr_ref, invn_ref, wuq_ref, wukvk_ref, wukvv_ref,
                   naq_o, nak_o, nav_o, gq_o, gk_o, gv_o, mq_o, mk_o, mv_o, lx_o, lg_o):
    tm = x_ref.shape[0]
    xt = x_ref[...]
    sh1 = mod_ref[0, 0:1, :]
    sc1 = mod_ref[0, 1:2, :]
    h = _rms(xt) * g1_ref[...] * (1.0 + sc1) + sh1
    u = jnp.dot(h.astype(BF16), win_ref[...], preferred_element_type=F32)

    bd = bd_ref[...]
    inv64 = 1.0 / HEAD_DIM
    ones64 = jnp.ones((tm, HEAD_DIM), F32)

    naq_o[...] = (_group_rms(u[:, _C_NAQ:_C_NAQ + 256], bd, inv64) * naqn_ref[...]).astype(BF16)
    nak_o[...] = (_group_rms(u[:, _C_NAK:_C_NAK + 256], bd, inv64) * nakn_ref[...]).astype(BF16)
    nav_o[...] = u[:, _C_NAV:_C_NAV + 256].astype(BF16)

    cq, slq, shq = _rope_tile(rq_ref, cq_ref, 256)
    gq = _rope(_group_rms(u[:, _C_GQ:_C_GQ + 256], bd, inv64) * gqn_ref[...], cq, slq, shq, 16)
    for hh in range(N_HEADS):
        gq_o[hh] = gq[:, 64 * hh:64 * hh + 64].astype(BF16)
    gk = _rope(_group_rms(u[:, _C_GK:_C_GK + 128], bd[:128, :128], inv64) * gkn_ref[...],
               cq[:, :128], slq[:, :128], shq[:, :128], 16)
    gv = u[:, _C_GV:_C_GV + 128]
    for hh in range(GQA_KV_HEADS):
        gk_o[hh] = gk[:, 64 * hh:64 * hh + 64].astype(BF16)
        gv_o[hh] = jnp.concatenate([gv[:, 64 * hh:64 * hh + 64], ones64], axis=-1).astype(BF16)

    gm = gm_ref[...]
    invn = invn_ref[...]
    cm, slm, shm = _rope_tile(rm_ref, cm_ref, MLA_PAD)
    cqn =_rms(u[:, _C_MCQ:_C_MCQ + 256]) * qag_ref[...]
    mq = jnp.dot(cqn.astype(BF16), wuq_ref[...], preferred_element_type=F32)
    mq = _group_rms(mq, gm, invn) * mqn_ref[...]
    mq = _rope(mq, _tile4(cm), _tile4(slm), _tile4(shm), 8)
    ckvn = (_rms(u[:, _C_MCKV:_C_MCKV + 128]) * kvag_ref[...]).astype(BF16)
    mkn = jnp.dot(ckvn, wukvk_ref[...], preferred_element_type=F32)
    mkn = _group_rms(mkn, gm, invn) * mknn_ref[...]
    mvv = jnp.dot(ckvn, wukvv_ref[...], preferred_element_type=F32)
    kr = _group_rms(u[:, _C_MKR:_C_MKR + 128], gm[:128, :128], invn[:, :128]) * mknr_ref[...]
    kr = _rope(kr, cm, slm, shm, 8)
    for hh in range(N_HEADS):
        mq_o[hh] = mq[:, 128 * hh:128 * hh + 128].astype(BF16)
        mk_o[hh] = (mkn[:, 128 * hh:128 * hh + 128] + kr).astype(BF16)
        mv_o[hh] = jnp.concatenate([mvv[:, 64 * hh:64 * hh + 64], ones64], axis=-1).astype(BF16)

    lx_o[...] = u[:, _C_LX:_C_LX + 256]
    g = u[:, _C_LG:_C_LG + 256]
    lg_o[...] = 0.5 * g * (1.0 + jnp.tanh(0.7978845608028654 * (g + 0.044715 * g * g * g)))


def _premix(x, mod, S, tm, consts, lw):
    T, D = x.shape
    nL = S // tm
    row = lambda w: pl.BlockSpec((tm, w), lambda i: (i, 0))
    hm = lambda hn, w: pl.BlockSpec((hn, tm, w), lambda i: (0, i, 0))
    in_specs = [
        row(D),
        pl.BlockSpec((1, 6, D), lambda i: (jnp.where(i >= nL, 1, 0), 0, 0)),
        _full((1, D)), _full((D, IN_COLS_R)), _full((256, 256)), _full((512, 512)),
        pl.BlockSpec((1, tm // GRID_W, 768), lambda i: (i, 0, 0)),
        pl.BlockSpec((1, tm, 768), lambda i: (jnp.where(i >= nL, 1, 0), 0, 0)),
        pl.BlockSpec((1, tm // GRID_W, 3 * MLA_PAD), lambda i: (i, 0, 0)),
        pl.BlockSpec((1, tm, 3 * MLA_PAD), lambda i: (jnp.where(i >= nL, 1, 0), 0, 0)),
        _full((1, 256)), _full((1, 256)), _full((1, 256)), _full((1, 128)),
        _full((1, 256)), _full((1, 128)),
        _full((1, 512)), _full((1, 512)), _full((1, 128)), _full((1, 512)),
        _full((256, 512)), _full((128, 512)), _full((128, 256)),
    ]
    out_specs = [row(256), row(256), row(256),
                 hm(4, 64), hm(2, 64), hm(2, 128),
                 hm(4, 128), hm(4, 128), hm(4, 128),
                 row(256), row(256)]
    sds = jax.ShapeDtypeStruct
    out_shape = [sds((T, 256), BF16), sds((T, 256), BF16), sds((T, 256), BF16),
                 sds((4, T, 64), BF16), sds((2, T, 64), BF16), sds((2, T, 128), BF16),
                 sds((4, T, 128), BF16), sds((4, T, 128), BF16), sds((4, T, 128), BF16),
                 sds((T, 256), F32), sds((T, 256), F32)]
    return pl.pallas_call(
        _premix_kernel, grid=(T // tm,), in_specs=in_specs, out_specs=out_specs,
        out_shape=out_shape, compiler_params=_params(1), name="premix",
    )(x, mod, lw["g1"], lw["w_in"], consts["bd64"], consts["gm"],
      consts["rope_q_row"], consts["rope_q_col"], consts["rope_m_row"], consts["rope_m_col"],
      lw["na_qn"], lw["na_kn"], lw["gqa_qn"], lw["gqa_kn"], lw["qa_g"], lw["kva_g"],
      lw["mla_qn"], lw["mla_knn"], lw["mla_knr"], consts["invn"],
      lw["wuq"], lw["wukv_k"], lw["wukv_v"])


def _flash_kernel(q_ref, k_ref, v_ref, o_ref, m_scr, acc_scr, *, group, rb):
    j = pl.program_id(1)
    n_kv = k_ref.shape[0]
    tq = q_ref.shape[1]

    @pl.when(j == 0)
    def _():
        m_scr[...] = jnp.full(m_scr.shape, NEG, F32)
        acc_scr[...] = jnp.zeros(acc_scr.shape, F32)

    chains = [(hq, slice(r0, r0 + rb)) for hq in range(n_kv * group) for r0 in range(0, tq, rb)]

    def scores(chain):
        hq, rows = chain
        return lax.dot_general(q_ref[hq, rows, :], k_ref[hq // group], (((1,), (1,)), ((), ())),
                               preferred_element_type=F32)

    s = scores(chains[0])
    for n, (hq, rows) in enumerate(chains):
        s_next = scores(chains[n + 1]) if n + 1 < len(chains) else None
        m_old = m_scr[hq, rows, :]
        m_new = jnp.maximum(m_old, jnp.max(s, axis=-1, keepdims=True))
        alpha = jnp.exp2(m_old - m_new)
        p = jnp.exp2(s - m_new).astype(BF16)
        acc_scr[hq, rows, :] = (alpha * acc_scr[hq, rows, :]
                                + jnp.dot(p, v_ref[hq // group], preferred_element_type=F32))
        m_scr[hq, rows, :] = m_new
        s = s_next

    @pl.when(j == pl.num_programs(1) - 1)
    def _():
        outs = []
        for hq in range(n_kv * group):
            a = acc_scr[hq]
            outs.append(a[:, :HEAD_DIM] / a[:, HEAD_DIM:HEAD_DIM + 1])
        o_ref[...] = jnp.concatenate(outs, axis=-1)


def _flash(q, k, v, tq, tk, n_q, q_blk0, n_kv, kv_blk0):
    hq, _, d = q.shape
    hk = k.shape[0]
    group = hq // hk
    return pl.pallas_call(
        functools.partial(_flash_kernel, group=group, rb=min(FLASH_ROW_BLOCK, tq)),
        grid=(n_q, n_kv),
        in_specs=[pl.BlockSpec((hq, tq, d), lambda i, j: (0, q_blk0 + i, 0)),
                  pl.BlockSpec((hk, tk, d), lambda i, j: (0, kv_blk0 + j, 0)),
                  pl.BlockSpec((hk, tk, 128), lambda i, j: (0, kv_blk0 + j, 0))],
        out_specs=pl.BlockSpec((tq, hq * HEAD_DIM), lambda i, j: (i, 0)),
        out_shape=jax.ShapeDtypeStruct((n_q * tq, hq * HEAD_DIM), F32),
        scratch_shapes=[pltpu.VMEM((hq, tq, 1), F32), pltpu.VMEM((hq, tq, 128), F32)],
        compiler_params=_params(2), name="flash_attn",
    )(q, k, v)


def _na_kernel(q_ref, k_ref, v_ref, tab_ref, o_ref, *, rows, n_ctx):
    i = pl.program_id(0)
    s_lat = rows * GRID_W
    kc = k_ref[pl.ds(s_lat, n_ctx), :]
    vc = v_ref[pl.ds(s_lat, n_ctx), :]
    nq = N_HEADS * GRID_W
    same_head = (lax.broadcasted_iota(jnp.int32, (nq, GROUP_W), 0) // GRID_W
                 == lax.broadcasted_iota(jnp.int32, (nq, GROUP_W), 1) // HEAD_DIM)
    q = q_ref[...].astype(F32)
    q_bd = jnp.where(same_head, jnp.concatenate([q] * N_HEADS, axis=0), 0.0).astype(BF16)
    nt = (((1,), (1,)), ((), ()))
    s_cx = lax.dot_general(q_bd, kc, nt, preferred_element_type=F32)

    def heads_out(res):
        res = jnp.where(same_head, res, 0.0)
        out = res[0:GRID_W]
        for h in range(1, N_HEADS):
            out = out + res[GRID_W * h:GRID_W * (h + 1)]
        return out

    @pl.when(i < rows)
    def _():
        r0 = jnp.clip(i - NA_KH // 2, 0, rows - NA_KH)
        start = pl.multiple_of(r0 * GRID_W, GRID_W)
        ks = k_ref[pl.ds(start, NA_KH * GRID_W), :]
        vs = v_ref[pl.ds(start, NA_KH * GRID_W), :]
        s_nb = (lax.dot_general(q_bd, ks, nt, preferred_element_type=F32)
                + tab_ref[i - r0].reshape(nq, NA_KH * GRID_W))
        m = jnp.maximum(jnp.max(s_nb, axis=-1, keepdims=True), jnp.max(s_cx, axis=-1, keepdims=True))
        p_nb = jnp.exp(s_nb - m)
        p_cx = jnp.exp(s_cx - m)
        l = jnp.sum(p_nb, axis=-1, keepdims=True) + jnp.sum(p_cx, axis=-1, keepdims=True)
        res = (jnp.dot(p_nb.astype(BF16), vs, preferred_element_type=F32)
               + jnp.dot(p_cx.astype(BF16), vc, preferred_element_type=F32))
        o_ref[...] = heads_out(res / l)

    @pl.when(i >= rows)
    def _():
        m = jnp.max(s_cx, axis=-1, keepdims=True)
        p_cx = jnp.exp(s_cx - m)
        l = jnp.sum(p_cx, axis=-1, keepdims=True)
        res = jnp.dot(p_cx.astype(BF16), vc, preferred_element_type=F32)
        o_ref[...] = heads_out(res / l)


def _neighbourhood(q, k, v, tab, S):
    T = q.shape[0]
    rows = S // GRID_W
    n_ctx = T - S
    resident = lambda shape: pl.BlockSpec(shape, lambda i: (0,) * len(shape),
                                          pipeline_mode=pl.Buffered(1))
    return pl.pallas_call(
        functools.partial(_na_kernel, rows=rows, n_ctx=n_ctx),
        grid=(T // GRID_W,),
        in_specs=[pl.BlockSpec((GRID_W, 256), lambda i: (i, 0)),
                  resident((T, 256)), resident((T, 256)),
                  resident((NA_KH, N_HEADS, GRID_W, NA_KH * GRID_W))],
        out_specs=pl.BlockSpec((GRID_W, 256), lambda i: (i, 0)),
        out_shape=jax.ShapeDtypeStruct((T, 256), F32),
        compiler_params=_params(1), name="nbr_attn",
    )(q, k, v, tab)


def _lru_kernel(x_ref, xp_ref, xn_ref, cw_ref, cb_ref, wa_ref, ba_ref, wi_ref, bi_ref,
                lam_ref, y_ref, h_scr, *, reverse, n_lat, n_ctx):
    j = pl.program_id(0)
    tc = x_ref.shape[0]
    n_all = n_lat + n_ctx
    if reverse:
        chunk = jnp.where(j < n_ctx, n_all - 1 - j, n_all - 1 - j)
    else:
        chunk = jnp.where(j < n_ctx, n_lat + j, j - n_ctx)
    has_prev = jnp.logical_and(chunk != 0, chunk != n_lat)
    has_next = jnp.logical_and(chunk != n_lat - 1, chunk != n_all - 1)

    @pl.when(j == 0)
    def _():
        h_scr[...] = jnp.zeros(h_scr.shape, F32)

    xp = jnp.where(has_prev, xp_ref[...], 0.0)
    xn = jnp.where(has_next, xn_ref[...], 0.0)
    xe = jnp.concatenate([xp, x_ref[...], xn], axis=0)
    ne = tc + 16
    cw = cw_ref[...]
    xc = (cw[0:1] * pltpu.roll(xe, 1, 0)[8:8 + tc]
          + cw[1:2] * xe[8:8 + tc]
          + cw[2:3] * pltpu.roll(xe, ne - 1, 0)[8:8 + tc]
          + cw[3:4] * pltpu.roll(xe, ne - 2, 0)[8:8 + tc]
          + cb_ref[...])

    r = jax.nn.sigmoid(jnp.dot(xc, wa_ref[0], preferred_element_type=F32, precision=HIGHEST)
                       + ba_ref[0])
    gi = jax.nn.sigmoid(jnp.dot(xc, wi_ref[0], preferred_element_type=F32, precision=HIGHEST)
                        + bi_ref[0])
    z = -lam_ref[0]
    softplus = jnp.maximum(z, 0.0) + jnp.log1p(jnp.exp(-jnp.abs(z)))
    log_a = -LRU_C * r * softplus
    a = jnp.exp(log_a)
    b = jnp.sqrt(-jnp.tanh(log_a) * (a * a + 1.0)) * (gi * xc)

    rowi = lax.broadcasted_iota(jnp.int32, (tc, 1), 0)
    s = 1
    while s < tc:
        if reverse:
            keep = rowi < tc - s
            a_s = jnp.where(keep, pltpu.roll(a, tc - s, 0), 1.0)
            b_s = jnp.where(keep, pltpu.roll(b, tc - s, 0), 0.0)
        else:
            keep = rowi >= s
            a_s = jnp.where(keep, pltpu.roll(a, s, 0), 1.0)
            b_s = jnp.where(keep, pltpu.roll(b, s, 0), 0.0)
        b = a * b_s + b
        a = a * a_s
        s *= 2
    hcur = b + a * h_scr[...]
    y_ref[...] = hcur
    h_scr[...] = hcur[0:1] if reverse else hcur[tc - 1:tc]


def _lru_scan(lx, lw, S, tc, reverse):
    T = lx.shape[0]
    n_lat, n_ctx = S // tc, (T - S) // tc
    n_all = n_lat + n_ctx
    d = 1 if reverse else 0
    t8 = tc // 8

    def chunk_of(j):
        if reverse:
            return n_all - 1 - j
        return jnp.where(j < n_ctx, n_lat + j, j - n_ctx)

    dspec = lambda shape: pl.BlockSpec((1,) + shape, lambda j: (d,) + (0,) * len(shape))
    return pl.pallas_call(
        functools.partial(_lru_kernel, reverse=reverse, n_lat=n_lat, n_ctx=n_ctx),
        grid=(n_all,),
        in_specs=[pl.BlockSpec((tc, 256), lambda j: (chunk_of(j), 0)),
                  pl.BlockSpec((8, 256), lambda j: (jnp.maximum(chunk_of(j) * t8 - 1, 0), 0)),
                  pl.BlockSpec((8, 256),
                               lambda j: (jnp.minimum((chunk_of(j) + 1) * t8, T // 8 - 1), 0)),
                  _full((4, 256)), _full((1, 256)),
                  dspec((256, 256)), dspec((1, 256)), dspec((256, 256)), dspec((1, 256)),
                  dspec((1, 256))],
        out_specs=pl.BlockSpec((tc, 256), lambda j: (chunk_of(j), 0)),
        out_shape=jax.ShapeDtypeStruct((T, 256), F32),
        scratch_shapes=[pltpu.VMEM((1, 256), F32)],
        compiler_params=_params(1), name="rglru_bwd" if reverse else "rglru_fwd",
    )(lx, lx, lx, lw["conv_w"], lw["conv_b"], lw["wa"], lw["ba"], lw["wi"], lw["bi"], lw["lam"])


def _merge_kernel(x_ref, mod_ref, oa_ref, obl_ref, obc_ref, ocl_ref, occ_ref, yf_ref, yb_ref,
                  lg_ref, gg_ref, wout_ref, g2_ref, rw_ref, rb_ref, lt_ref, ut_ref,
                  x1_o, tok_o, slot_o, gate_o, cnt_o, *, n_lat_tiles):
    i = pl.program_id(0)
    tm = x_ref.shape[0]
    is_ctx = i >= n_lat_tiles

    gg = gg_ref[...]
    od = (yf_ref[...] + yb_ref[...]) * lg_ref[...]
    ob = jnp.where(is_ctx, obc_ref[...], obl_ref[...])
    oc = jnp.where(is_ctx, occ_ref[...], ocl_ref[...])
    parts = [oa_ref[...], ob, oc, od]
    ycat = jnp.concatenate(
        [(_rms(p) * gg[:, GROUP_W * n:GROUP_W * (n + 1)]).astype(BF16) for n, p in enumerate(parts)],
        axis=-1)
    y = jnp.dot(ycat, wout_ref[...], preferred_element_type=F32)
    gt1 = mod_ref[0, 2:3, :]
    sh2 = mod_ref[0, 3:4, :]
    sc2 = mod_ref[0, 4:5, :]
    x1 = x_ref[...] + gt1 * y
    x1_o[...] = x1
    tok = _rms(x1) * g2_ref[...] * (1.0 + sc2) + sh2
    tok_o[...] = tok.astype(BF16)

    logits = jnp.dot(tok, rw_ref[...], preferred_element_type=F32, precision=HIGHEST) + rb_ref[...]
    lane = lax.broadcasted_iota(jnp.int32, (tm, LANES), 1)
    v_acc = jnp.zeros((tm, LANES), F32)
    onehots = []
    v0 = None
    work = logits
    for k in range(TOP_K):
        vk = jnp.max(work, axis=-1, keepdims=True)
        ek = jnp.min(jnp.where(work == vk, lane, LANES), axis=-1, keepdims=True)
        sel = lane == ek
        onehots.append(sel)
        work = jnp.where(sel, NEG * 2.0, work)
        if k == 0:
            v0 = vk
        v_acc = jnp.where(lane == k, jnp.exp(vk - v0), v_acc)
    gate_o[...] = v_acc / jnp.sum(v_acc, axis=-1, keepdims=True)

    oh = [jnp.where(o, 1.0, 0.0) for o in onehots]
    oh_all = oh[0] + oh[1] + oh[2] + oh[3]
    cnt = jnp.sum(oh_all, axis=0, keepdims=True)
    run = jnp.floor((cnt + (RUN_ALIGN - 1.0)) * (1.0 / RUN_ALIGN)) * RUN_ALIGN
    run_start = jnp.dot(jnp.broadcast_to(run, (8, LANES)).astype(BF16), ut_ref[...],
                        preferred_element_type=F32)[0:1]
    excl = jnp.dot(lt_ref[...], oh_all.astype(BF16), preferred_element_type=F32) + run_start
    slot = jnp.full((tm, LANES), -1.0, F32)
    for k in range(TOP_K):
        sk = jnp.sum(oh[k] * excl, axis=-1, keepdims=True)
        slot = jnp.where(lane == k, sk, slot)
    slot_o[...] = slot.astype(jnp.int32)
    cnt_o[0] = jnp.broadcast_to(cnt, (8, LANES))


def _merge(x, mod, oa, ob_lat, ob_ctx, oc_lat, oc_ctx, yf, yb, lg, S, tm, consts, lw):
    T, D = x.shape
    nL = S // tm
    nt = T // tm
    row = lambda w: pl.BlockSpec((tm, w), lambda i: (i, 0))
    lat = pl.BlockSpec((tm, 256), lambda i: (jnp.minimum(i, nL - 1), 0))
    ctx = pl.BlockSpec((tm, 256), lambda i: (jnp.maximum(i - nL, 0), 0))
    sds = jax.ShapeDtypeStruct
    return pl.pallas_call(
        functools.partial(_merge_kernel, n_lat_tiles=nL), grid=(nt,),
        in_specs=[row(D), pl.BlockSpec((1, 6, D), lambda i: (jnp.where(i >= nL, 1, 0), 0, 0)),
                  row(256), lat, ctx, lat, ctx, row(256), row(256), row(256),
                  _full((1, D)), _full((D, D)), _full((1, D)), _full((D, LANES)),
                  _full((1, LANES)), _full((tm, tm)), _full((LANES, LANES))],
        out_specs=[row(D), row(D), row(LANES), row(LANES),
                   pl.BlockSpec((1, 8, LANES), lambda i: (i, 0, 0))],
        out_shape=[sds((T, D), F32), sds((T, D), BF16), sds((T, LANES), jnp.int32),
                   sds((T, LANES), F32), sds((nt, 8, LANES), F32)],
        compiler_params=_params(1), name="merge_router",
    )(x, mod, oa, ob_lat, ob_ctx, oc_lat, oc_ctx, yf, yb, lg, lw["grp_g"], lw["w_out"], lw["g2"],
      lw["router_w"], lw["router_b"], consts["lt"], consts["ut"])


def _routing_meta(cnt, bm, n_rows):
    c = cnt[:, 0, :N_EXPERTS].astype(jnp.int32)
    nt = c.shape[0]
    run = (c + RUN_ALIGN - 1) // RUN_ALIGN * RUN_ALIGN
    tile_off = jnp.cumsum(run, axis=0) - run
    tot = jnp.sum(run, axis=0)
    padded = (tot + bm - 1) // bm * bm
    seg_end = jnp.cumsum(padded)
    seg_start = seg_end - padded
    g_start = seg_start[None, :] + tile_off
    l_start = jnp.cumsum(run, axis=1) - run
    per_size, done = [], jnp.zeros_like(run)
    for size in CHUNK_ROWS:
        n = (run - done) // size
        per_size.append(jnp.sum(n, axis=1, keepdims=True))
        done = done + n * size
    n_tot = jnp.sum(run, axis=1, keepdims=True) // RUN_ALIGN
    meta = jnp.concatenate(
        [g_start, l_start, run] + per_size
        + [n_tot, jnp.zeros((nt, LANES - 3 * N_EXPERTS - len(CHUNK_ROWS) - 1), jnp.int32)], axis=1)
    n_zero = (padded - tot) // RUN_ALIGN
    zmeta = jnp.concatenate([seg_start + tot, n_zero, jnp.sum(n_zero)[None], seg_end[-1:] // bm,
                             jnp.zeros((LANES - 2 * N_EXPERTS - 2,), jnp.int32)])
    blk_row = jnp.arange(n_rows // bm, dtype=jnp.int32) * bm
    blk_expert = jnp.minimum(jnp.sum((seg_end[None, :] <= blk_row[:, None]).astype(jnp.int32), axis=1),
                             N_EXPERTS - 1)
    n_used = seg_end[-1:] // bm
    return (meta.reshape(nt, 1, LANES).astype(jnp.int32), zmeta.reshape(1, 1, LANES).astype(jnp.int32),
            blk_expert.astype(jnp.int32), n_used.astype(jnp.int32))


def _run_copies(meta_ref, make_copy):
    def per_expert(e, c):
        g = meta_ref[0, 0, e]
        l = meta_ref[0, 0, N_EXPERTS + e]
        rows = meta_ref[0, 0, 2 * N_EXPERTS + e]
        done = 0
        for size in CHUNK_ROWS:
            n = (rows - done) // size
            start = done

            def issue(cc, c2, size=size, start=start):
                off = start + size * cc
                make_copy(pl.multiple_of(g + off, RUN_ALIGN), pl.multiple_of(l + off, RUN_ALIGN),
                          size).start()
                return c2

            c = lax.fori_loop(0, n, issue, c)
            done = done + n * size
        return c

    lax.fori_loop(0, N_EXPERTS, per_expert, 0)


def _drain(meta_ref, make_copy):
    for n, size in enumerate(CHUNK_ROWS):
        def body(_, c, size=size):
            make_copy(0, 0, size).wait()
            return c

        lax.fori_loop(0, meta_ref[0, 0, 3 * N_EXPERTS + n], body, 0)


def _dispatch_kernel(meta_ref, prev_meta_ref, zmeta_ref, slot_ref, tok_ref, xin_ref,
                     xs_scr, z_scr, sem):
    i = pl.program_id(0)
    tm = tok_ref.shape[0]
    n_slot = xs_scr.shape[1]
    buf = lax.rem(i, 2)
    slot_t = jnp.transpose(slot_ref[...].astype(F32))
    srow = lax.broadcasted_iota(jnp.int32, (n_slot, tm), 0).astype(F32)
    perm = jnp.zeros((n_slot, tm), F32)
    for k in range(TOP_K):
        perm = perm + jnp.where(srow == slot_t[k:k + 1, :], 1.0, 0.0)
    xs_scr[buf] = jnp.dot(perm.astype(BF16), tok_ref[...], preferred_element_type=F32)

    def to_global(g_row, l_row, size):
        return pltpu.make_async_copy(xs_scr.at[buf, pl.ds(l_row, size)],
                                     xin_ref.at[pl.ds(g_row, size)], sem)

    @pl.when(i > 0)
    def _():
        _drain(prev_meta_ref, to_global)

    _run_copies(meta_ref, to_global)

    @pl.when(i == pl.num_programs(0) - 1)
    def _():
        _drain(meta_ref, to_global)
        z_scr[...] = jnp.zeros(z_scr.shape, F32)

        def zero_copy(g_row):
            return pltpu.make_async_copy(z_scr.at[pl.ds(0, RUN_ALIGN)],
                                         xin_ref.at[pl.ds(g_row, RUN_ALIGN)], sem)

        def per_expert(e, c):
            g = zmeta_ref[0, 0, e]

            def issue(cc, c2):
                zero_copy(pl.multiple_of(g + RUN_ALIGN * cc, RUN_ALIGN)).start()
                return c2

            return lax.fori_loop(0, zmeta_ref[0, 0, N_EXPERTS + e], issue, c)

        lax.fori_loop(0, N_EXPERTS, per_expert, 0)

        def drain(_, c):
            zero_copy(0).wait()
            return c

        lax.fori_loop(0, zmeta_ref[0, 0, 2 * N_EXPERTS], drain, 0)

        bm = z_scr.shape[0]

        def block_copy(b):
            return pltpu.make_async_copy(z_scr, xin_ref.at[pl.ds(pl.multiple_of(b * bm, bm), bm)], sem)

        def issue_block(b, c):
            block_copy(b).start()
            return c

        def drain_block(b, c):
            block_copy(b).wait()
            return c

        n_used = zmeta_ref[0, 0, 2 * N_EXPERTS + 1]
        lax.fori_loop(n_used, xin_ref.shape[0] // bm, issue_block, 0)
        lax.fori_loop(n_used, xin_ref.shape[0] // bm, drain_block, 0)


def _dispatch(tok, slot, meta, zmeta, n_rows, tm, bm):
    T, D = tok.shape
    n_slot = TOP_K * tm + N_EXPERTS * RUN_ALIGN
    smem = lambda idx: pl.BlockSpec((1, 1, LANES), idx, memory_space=pltpu.SMEM)
    return pl.pallas_call(
        _dispatch_kernel,
        grid=(T // tm,),
        in_specs=[smem(lambda i: (i, 0, 0)), smem(lambda i: (jnp.maximum(i - 1, 0), 0, 0)),
                  smem(lambda i: (0, 0, 0)),
                  pl.BlockSpec((tm, LANES), lambda i: (i, 0)),
                  pl.BlockSpec((tm, D), lambda i: (i, 0))],
        out_specs=pl.BlockSpec(memory_space=pl.ANY),
        out_shape=jax.ShapeDtypeStruct((n_rows, D), F32),
        scratch_shapes=[pltpu.VMEM((2, n_slot, D), F32), pltpu.VMEM((bm, D), F32),
                        pltpu.SemaphoreType.DMA(())],
        compiler_params=_params(1), name="moe_dispatch",
    )(meta, meta, zmeta, slot, tok)


def _expert_kernel(be_ref, nu_ref, x_ref, wgu_ref, bgu_ref, wd_ref, bd_ref, y_ref):
    i = pl.program_id(0)
    d_e = wd_ref.shape[1]

    @pl.when(i < nu_ref[0])
    def _():
        gu = jnp.dot(x_ref[...].astype(BF16), wgu_ref[0], preferred_element_type=F32) + bgu_ref[0]
        x_glu = jnp.minimum(gu[:, :d_e], SWIGLU_LIMIT)
        x_lin = jnp.clip(gu[:, d_e:], -SWIGLU_LIMIT, SWIGLU_LIMIT)
        act = x_glu * jax.nn.sigmoid(SWIGLU_ALPHA * x_glu) * (x_lin + 1.0)
        y_ref[...] = jnp.dot(act.astype(BF16), wd_ref[0], preferred_element_type=F32) + bd_ref[0]

    @pl.when(i >= nu_ref[0])
    def _():
        y_ref[...] = jnp.zeros(y_ref.shape, F32)


def _experts(xin, blk_expert, n_used, layer, w_gu, b_gu, w_down, b_down, bm):
    P, D = xin.shape
    d2 = w_gu.shape[-1]
    d_e = d2 // 2
    blk = lambda i, be, nu: jnp.minimum(i, nu[0] - 1)
    wspec = lambda r, c: pl.BlockSpec((None, 1, r, c),
                                      lambda i, be, nu: (layer, be[blk(i, be, nu)], 0, 0))
    grid_spec = pltpu.PrefetchScalarGridSpec(
        num_scalar_prefetch=2, grid=(P // bm,),
        in_specs=[pl.BlockSpec((bm, D), lambda i, be, nu: (blk(i, be, nu), 0)),
                  wspec(D, d2), wspec(1, d2), wspec(d_e, D), wspec(1, D)],
        out_specs=pl.BlockSpec((bm, D), lambda i, be, nu: (i, 0)))
    return pl.pallas_call(
        _expert_kernel, grid_spec=grid_spec,
        out_shape=jax.ShapeDtypeStruct((P, D), F32),
        compiler_params=_params(1), name="moe_experts",
    )(blk_expert, n_used, xin, w_gu, b_gu, w_down, b_down)


def _combine_kernel(meta_ref, next_meta_ref, slot_ref, gate_ref, y_ref, x_ref, mod_ref, o_ref,
                    ybuf, sem):
    i = pl.program_id(0)
    tm = x_ref.shape[0]
    _, n_slot, d = ybuf.shape
    buf = lax.rem(i, 2)

    def fetch(m_ref, b):
        def to_local(g_row, l_row, size):
            return pltpu.make_async_copy(y_ref.at[pl.ds(g_row, size)],
                                         ybuf.at[b, pl.ds(l_row, size)], sem)

        _run_copies(m_ref, to_local)
        n_tot = m_ref[0, 0, 3 * N_EXPERTS + len(CHUNK_ROWS)]

        def zero_tail(cc, c):
            ybuf[b, pl.ds(pl.multiple_of(RUN_ALIGN * cc, RUN_ALIGN), RUN_ALIGN), :] = jnp.zeros(
                (RUN_ALIGN, d), F32)
            return c

        lax.fori_loop(n_tot, n_slot // RUN_ALIGN, zero_tail, 0)
        return to_local

    @pl.when(i == 0)
    def _():
        fetch(meta_ref, buf)

    def wait_shape(g_row, l_row, size):
        return pltpu.make_async_copy(y_ref.at[pl.ds(g_row, size)],
                                     ybuf.at[buf, pl.ds(l_row, size)], sem)

    _drain(meta_ref, wait_shape)

    @pl.when(i + 1 < pl.num_programs(0))
    def _():
        fetch(next_meta_ref, 1 - buf)

    slot = slot_ref[...]
    gate = gate_ref[...]
    col = lax.broadcasted_iota(jnp.int32, (tm, n_slot), 1)
    w = jnp.zeros((tm, n_slot), F32)
    for k in range(TOP_K):
        w = w + jnp.where(col == slot[:, k:k + 1], gate[:, k:k + 1], 0.0)
    w_hi = w.astype(BF16)
    w_lo = (w - w_hi.astype(F32)).astype(BF16)
    yb = ybuf[buf].astype(BF16)
    f = (jnp.dot(w_hi, yb, preferred_element_type=F32)
         + jnp.dot(w_lo, yb, preferred_element_type=F32))
    o_ref[...] = x_ref[...] + mod_ref[0, 5:6, :] * f


def _combine(y, slot, gate, meta, x1, mod, S, tm):
    T, D = x1.shape
    nL = S // tm
    n_slot = TOP_K * tm + N_EXPERTS * RUN_ALIGN
    return pl.pallas_call(
        _combine_kernel,
        grid=(T // tm,),
        in_specs=[pl.BlockSpec((1, 1, LANES), lambda i: (i, 0, 0), memory_space=pltpu.SMEM),
                  pl.BlockSpec((1, 1, LANES), lambda i: (jnp.minimum(i + 1, T // tm - 1), 0, 0),
                               memory_space=pltpu.SMEM),
                  pl.BlockSpec((tm, LANES), lambda i: (i, 0)),
                  pl.BlockSpec((tm, LANES), lambda i: (i, 0)),
                  pl.BlockSpec(memory_space=pl.ANY),
                  pl.BlockSpec((tm, D), lambda i: (i, 0)),
                  pl.BlockSpec((1, 6, D), lambda i: (jnp.where(i >= nL, 1, 0), 0, 0))],
        out_specs=pl.BlockSpec((tm, D), lambda i: (i, 0)),
        out_shape=jax.ShapeDtypeStruct((T, D), F32),
        scratch_shapes=[pltpu.VMEM((2, n_slot, D), F32), pltpu.SemaphoreType.DMA(())],
        compiler_params=_params(1), name="moe_combine",
    )(meta, meta, slot, gate, y, x1, mod)


def _rope_tables(S, T, tm, rot_dim, width, lane0, reps):
    h = rot_dim // 4
    ax = rot_dim // 2
    inv = ROPE_THETA ** (-jnp.arange(0, ax, 2, dtype=F32) / ax)

    def parts(n):
        ang = jnp.arange(n, dtype=F32)[:, None] * inv
        return jnp.cos(ang), jnp.sin(ang), jnp.zeros((n, h), F32)

    def place(blocks, fill):
        n = blocks[0].shape[0]
        tab = jnp.concatenate(
            [jnp.full((n, lane0), fill, F32)] + list(blocks)
            + [jnp.full((n, width - lane0 - rot_dim), fill, F32)], axis=-1)
        return jnp.tile(tab, (1, reps))

    n_rows = S // GRID_W
    c, s, z = parts(n_rows)
    row = jnp.concatenate([place([c, c, z, z], 0.0), place([z, s, z, z], 0.0),
                           place([-s, z, z, z], 0.0)], axis=-1)
    row = jnp.concatenate([row, jnp.zeros(((T - S) // GRID_W, row.shape[1]), F32)], axis=0)
    c, s, z = parts(GRID_W)
    col = jnp.concatenate([place([z, z, c, c], 1.0), place([z, z, z, s], 0.0),
                           place([z, z, -s, z], 0.0)], axis=-1)
    col = jnp.tile(col, (tm // GRID_W, 1))
    ident = jnp.concatenate([jnp.ones((tm, reps * width), F32),
                             jnp.zeros((tm, 2 * reps * width), F32)], axis=-1)
    return row.reshape(T // tm, tm // GRID_W, -1), jnp.stack([col, ident])


def _constants(S, T, tm):
    rope_q_row, rope_q_col = _rope_tables(S, T, tm, HEAD_DIM, HEAD_DIM, 0, N_HEADS)
    rope_m_row, rope_m_col = _rope_tables(S, T, tm, MLA_ROPE, MLA_PAD, MLA_NOPE, 1)
    bd64 = np.kron(np.eye(4, dtype=np.float32), np.ones((64, 64), np.float32))
    sizes = [MLA_NOPE, MLA_ROPE, MLA_PAD - MLA_NOPE - MLA_ROPE] * N_HEADS
    gid = np.repeat(np.arange(len(sizes)), sizes)
    gm = (gid[:, None] == gid[None, :]).astype(np.float32)
    invn = (1.0 / np.repeat(np.asarray(sizes, np.float32), sizes))[None, :]
    lt = np.tril(np.ones((tm, tm), np.float32), -1)
    ut = np.triu(np.ones((LANES, LANES), np.float32), 1)
    return dict(rope_q_row=rope_q_row, rope_q_col=rope_q_col,
                rope_m_row=rope_m_row, rope_m_col=rope_m_col,
                bd64=jnp.asarray(bd64, BF16), gm=jnp.asarray(gm, BF16),
                invn=jnp.asarray(invn), lt=jnp.asarray(lt, BF16), ut=jnp.asarray(ut, BF16))


def _na_bias_tables(rpb):
    qc = np.arange(GRID_W)[:, None]
    kc = np.arange(GRID_W)[None, :]
    c0 = np.clip(qc - NA_KW // 2, 0, GRID_W - NA_KW)
    valid = (kc >= c0) & (kc < c0 + NA_KW)
    sel = ((kc - qc + NA_KW - 1)[:, :, None] == np.arange(2 * NA_KW - 1)) & valid[:, :, None]
    toep = jnp.einsum("lhab,qkb->lhaqk", rpb, jnp.asarray(sel, F32), precision=HIGHEST)
    toep = jnp.where(jnp.asarray(valid)[None, None, None], toep, NEG)
    L, H = rpb.shape[:2]
    tabs = []
    for off in range(NA_KH):
        rows = toep[:, :, NA_KH - 1 - off:2 * NA_KH - 1 - off]
        tabs.append(rows.transpose(0, 1, 3, 2, 4).reshape(L, H, GRID_W, NA_KH * GRID_W))
    return jnp.stack(tabs, axis=1)


def _block_diag(w):
    n, bw, _ = w.shape
    eye = jnp.eye(n, dtype=w.dtype)
    return (eye[:, None, :, None] * w[:, :, None, :]).reshape(n * bw, n * bw)


def _layer_weights(l, p):
    D = p["w_in"].shape[1]
    w_in = p["w_in"][l]
    offs = np.cumsum([0, 256, 256, 256, 256, 128, 128, 256, 128, 32, 256, 256])
    seg = lambda n: w_in[:, offs[n]:offs[n + 1]]
    z = lambda n: jnp.zeros((D, n), F32)
    w_in_r = jnp.concatenate(
        [seg(0), seg(1), seg(2), seg(3), seg(4), seg(5), seg(6), seg(7), seg(9), seg(10),
         z(MLA_NOPE), seg(8), z(MLA_PAD - MLA_NOPE - MLA_ROPE)], axis=-1).astype(BF16)
    sc = HEAD_DIM ** -0.5
    sc_m = (MLA_NOPE + MLA_ROPE) ** -0.5
    t4 = lambda g: jnp.tile(g, N_HEADS)[None, :]
    wuq = p["mla_wuq"][l].reshape(-1, N_HEADS, MLA_NOPE + MLA_ROPE)
    wuq = jnp.pad(wuq, ((0, 0), (0, 0), (0, MLA_PAD - MLA_NOPE - MLA_ROPE)))
    wukv = p["mla_wukv"][l].reshape(-1, N_HEADS, MLA_NOPE + HEAD_DIM)
    wukv_k = jnp.pad(wukv[:, :, :MLA_NOPE], ((0, 0), (0, 0), (0, MLA_PAD - MLA_NOPE)))
    qn, kn = p["mla_qn"][l], p["mla_kn"][l]
    padq = jnp.pad(qn * (sc_m * LOG2E), (0, MLA_PAD - MLA_NOPE - MLA_ROPE))
    padkn = jnp.pad(kn[:MLA_NOPE], (0, MLA_PAD - MLA_NOPE))
    padkr = jnp.pad(kn[MLA_NOPE:], (MLA_NOPE, MLA_PAD - MLA_NOPE - MLA_ROPE))
    rw =jnp.pad(p["router_w"][l], ((0, 0), (0, LANES - N_EXPERTS)))
    rb = jnp.pad(p["router_b"][l], (0, LANES - N_EXPERTS), constant_values=NEG)[None, :]
    return dict(
        g1=p["norm1_g"][l][None, :], g2=p["norm2_g"][l][None, :], w_in=w_in_r,
        na_qn=t4(p["na_qn"][l] * sc), na_kn=t4(p["na_kn"][l]),
        gqa_qn=t4(p["gqa_qn"][l] * (sc * LOG2E)), gqa_kn=jnp.tile(p["gqa_kn"][l], GQA_KV_HEADS)[None, :],
        qa_g=p["mla_qa_g"][l][None, :], kva_g=p["mla_kva_g"][l][None, :],
        mla_qn=t4(padq), mla_knn=t4(padkn), mla_knr=padkr[None, :],
        wuq=wuq.reshape(-1, N_HEADS * MLA_PAD).astype(BF16),
        wukv_k=wukv_k.reshape(-1, N_HEADS * MLA_PAD).astype(BF16),
        wukv_v=wukv[:, :, MLA_NOPE:].reshape(-1, N_HEADS * HEAD_DIM).astype(BF16),
        conv_w=p["lru_conv_w"][l][:, 0, :], conv_b=p["lru_conv_b"][l][None, :],
        wa=jnp.stack([_block_diag(p["lru_wa"][l][d]) for d in range(2)]),
        wi=jnp.stack([_block_diag(p["lru_wi"][l][d]) for d in range(2)]),
        ba=p["lru_ba"][l][:, None, :], bi=p["lru_bi"][l][:, None, :],
        lam=p["lru_lam"][l][:, None, :],
        grp_g=p["grp_g"][l][None, :], w_out=p["w_out"][l].astype(BF16),
        router_w=rw, router_b=rb,
    )


def _tiles(S, C):
    tm = min(256, C)
    tq = min(1024, S)
    T = S + C
    tk = next(t for t in (3328, 1280, 640, 256, 128) if T % t == 0)
    return dict(tm=tm, tq=tq, tk=tk, tc=min(256, C), bm=512)


def kernel(x, c, ctx, c_ctx, ada_w, ada_b, norm1_g, norm2_g, w_in, na_qn, na_kn, na_rpb, gqa_qn, gqa_kn, mla_qa_g, mla_kva_g, mla_wuq, mla_wukv, mla_qn, mla_kn, lru_conv_w, lru_conv_b, lru_wa, lru_ba, lru_wi, lru_bi, lru_lam, grp_g, w_out, router_w, router_b, exp_w_gu, exp_b_gu, exp_w_down, exp_b_down):
    p = dict(norm1_g=norm1_g, norm2_g=norm2_g, w_in=w_in, na_qn=na_qn, na_kn=na_kn, na_rpb=na_rpb,
             gqa_qn=gqa_qn, gqa_kn=gqa_kn, mla_qa_g=mla_qa_g, mla_kva_g=mla_kva_g,
             mla_wuq=mla_wuq, mla_wukv=mla_wukv, mla_qn=mla_qn, mla_kn=mla_kn,
             lru_conv_w=lru_conv_w, lru_conv_b=lru_conv_b, lru_wa=lru_wa, lru_ba=lru_ba,
             lru_wi=lru_wi, lru_bi=lru_bi, lru_lam=lru_lam, grp_g=grp_g, w_out=w_out,
             router_w=router_w, router_b=router_b, exp_w_gu=exp_w_gu, exp_b_gu=exp_b_gu,
             exp_w_down=exp_w_down, exp_b_down=exp_b_down)
    B, S, D = x.shape
    assert B == 1 and S % GRID_W == 0
    C = ctx.shape[1]
    T = S + C
    L = ada_w.shape[0]
    ts = _tiles(S, C)
    tm, bm = ts["tm"], ts["bm"]
    consts = _constants(S, T, tm)

    c8 = jnp.zeros((8, D), F32).at[0].set(c[0]).at[1].set(c_ctx)
    mods = _modulation(c8, ada_w, ada_b)[:, :2].reshape(L, 2, 6, D)

    n_rows = -(-(T * TOP_K + (T // tm) * N_EXPERTS * (RUN_ALIGN - 1) + N_EXPERTS * (bm - 1)) // bm) * bm
    na_tabs = _na_bias_tables(na_rpb)
    n_exp = exp_w_gu.shape[1]
    w_gu, w_down = exp_w_gu.astype(BF16), exp_w_down.astype(BF16)
    b_gu, b_down = exp_b_gu.reshape(L, n_exp, 1, -1), exp_b_down.reshape(L, n_exp, 1, -1)
    tq, tk = ts["tq"], ts["tk"]
    xs = jnp.concatenate([x[0], ctx[0]], axis=0)
    for l in range(L):
        lw = _layer_weights(l, p)
        mod = mods[l]
        (naq, nak, nav, gq, gk, gv, mq, mk, mv, lx, lg) = _premix(xs, mod, S, tm, consts, lw)
        oa = _neighbourhood(naq, nak, nav, na_tabs[l], S)
        ob_lat = _flash(gq, gk, gv, tq, tk, S // tq, 0, T // tk, 0)
        ob_ctx = _flash(gq, gk, gv, C, C, 1, S // C, 1, S // C)
        oc_lat = _flash(mq, mk, mv, tq, tk, S // tq, 0, T // tk, 0)
        oc_ctx = _flash(mq, mk, mv, C, C, 1, S // C, 1, S // C)
        yf = _lru_scan(lx, lw, S, ts["tc"], False)
        yb = _lru_scan(lx, lw, S, ts["tc"], True)
        x1, tok, slot, gate, cnt = _merge(xs, mod, oa, ob_lat, ob_ctx, oc_lat, oc_ctx, yf, yb, lg,
                                          S, tm, consts, lw)
        meta, zmeta, blk_expert, n_used = _routing_meta(cnt, bm, n_rows)
        xin = _dispatch(tok, slot, meta, zmeta, n_rows, tm, bm)
        y = _experts(xin, blk_expert, n_used, l, w_gu, b_gu, w_down, b_down, bm)
        xs = _combine(y, slot, gate, meta, x1, mod, S, tm)
    return xs[:S][None]
```

```python
import functools

import numpy as np
import jax
import jax.numpy as jnp
from jax import lax
from jax.experimental import pallas as pl
from jax.experimental.pallas import tpu as pltpu

F32 = jnp.float32
BF16 = jnp.bfloat16
HIGHEST = lax.Precision.HIGHEST

GRID_W = 64
HEAD_DIM = 64
N_HEADS = 4
GROUP_W = 256
GQA_KV_HEADS = 2
NA_KH = 8
NA_KW = 16
MLA_NOPE = 64
MLA_ROPE = 32
MLA_PAD = 128
LRU_C = 8.0
N_EXPERTS = 32
TOP_K = 4
SWIGLU_LIMIT = 7.0
SWIGLU_ALPHA = 1.702
ROPE_THETA = 10000.0
EPS = 1e-6
NEG = -1e30
LANES = 128
FLASH_ROW_BLOCK = 256
LOG2E = 1.4426950408889634
RUN_ALIGN = 8
CHUNK_ROWS = (32, 16, 8)
assert all(a == 2 * b for a, b in zip(CHUNK_ROWS, CHUNK_ROWS[1:])) and CHUNK_ROWS[-1] == RUN_ALIGN
VMEM_LIMIT = 56 * 1024 * 1024

_C_NAQ, _C_NAK, _C_NAV = 0, 256, 512
_C_GQ, _C_GK, _C_GV = 768, 1024, 1152
_C_MCQ, _C_MCKV = 1280, 1536
_C_LX, _C_LG = 1664, 1920
_C_MKR = 2176
IN_COLS_R = 2304


def _params(n_axes, vmem=VMEM_LIMIT):
    return pltpu.CompilerParams(dimension_semantics=("arbitrary",) * n_axes,
                                vmem_limit_bytes=vmem)


def _full(shape):
    n = len(shape)
    return pl.BlockSpec(shape, lambda *_: (0,) * n)


def _mod_kernel(c_ref, w_ref, b_ref, o_ref):
    cc = c_ref[...]
    s = cc * jax.nn.sigmoid(cc)
    o_ref[0] = jnp.dot(s, w_ref[0], preferred_element_type=F32, precision=HIGHEST) + b_ref[0]


def _modulation(c8, ada_w, ada_b):
    L, D, D6 = ada_w.shape
    tn = D6 // 6
    return pl.pallas_call(
        _mod_kernel,
        grid=(L, D6 // tn),
        in_specs=[pl.BlockSpec((8, D), lambda l, j: (0, 0)),
                  pl.BlockSpec((1, D, tn), lambda l, j: (l, 0, j)),
                  pl.BlockSpec((1, 1, tn), lambda l, j: (l, 0, j))],
        out_specs=pl.BlockSpec((1, 8, tn), lambda l, j: (l, 0, j)),
        out_shape=jax.ShapeDtypeStruct((L, 8, D6), F32),
        compiler_params=_params(2),
        name="adaln_mod",
    )(c8, ada_w, ada_b.reshape(L, 1, D6))


def _rms(z):
    return z * lax.rsqrt(jnp.mean(z * z, axis=-1, keepdims=True) + EPS)


def _group_rms(z, gmat, inv_n):
    zz = z * z
    hi = zz.astype(BF16)
    lo = (zz - hi.astype(F32)).astype(BF16)
    ss = (jnp.dot(hi, gmat, preferred_element_type=F32)
          + jnp.dot(lo, gmat, preferred_element_type=F32))
    return z * lax.rsqrt(ss * inv_n + EPS)


def _rope(z, cos, sin_lo, sin_hi, half):
    w = z.shape[-1]
    return z * cos + pltpu.roll(z, half, 1) * sin_lo + pltpu.roll(z, w - half, 1) * sin_hi


def _tile4(t):
    return jnp.concatenate([t, t, t, t], axis=-1)


def _rope_tile(row_ref, col_ref, width):
    rt = row_ref[0]
    rows = jnp.concatenate([jnp.broadcast_to(rt[g:g + 1], (GRID_W, rt.shape[1]))
                            for g in range(rt.shape[0])], axis=0)
    tab = rows + col_ref[0]
    return tab[:, :width], tab[:, width:2 * width], tab[:, 2 * width:]


def _premix_kernel(x_ref, mod_ref, g1_ref, win_ref, bd_ref, gm_ref,
                   rq_ref, cq_ref, rm_ref, cm_ref,
                   naqn_ref, nakn_ref, gqn_ref, gkn_ref, qag_ref, kvag_ref,
                   mqn_ref, mknn_ref, mknr_ref, invn_ref, wuq_ref, wukvk_ref, wukvv_ref,
                   naq_o, nak_o, nav_o, gq_o, gk_o, gv_o, mq_o, mk_o, mv_o, lx_o, lg_o):
    tm = x_ref.shape[0]
    xt = x_ref[...]
    sh1 = mod_ref[0, 0:1, :]
    sc1 = mod_ref[0, 1:2, :]
    h = _rms(xt) * g1_ref[...] * (1.0 + sc1) + sh1
    u = jnp.dot(h.astype(BF16), win_ref[...], preferred_element_type=F32)

    bd = bd_ref[...]
    inv64 = 1.0 / HEAD_DIM
    ones64 = jnp.ones((tm, HEAD_DIM), F32)

    naq_o[...] = (_group_rms(u[:, _C_NAQ:_C_NAQ + 256], bd, inv64) * naqn_ref[...]).astype(BF16)
    nak_o[...] = (_group_rms(u[:, _C_NAK:_C_NAK + 256], bd, inv64) * nakn_ref[...]).astype(BF16)
    nav_o[...] = u[:, _C_NAV:_C_NAV + 256].astype(BF16)

    cq, slq, shq = _rope_tile(rq_ref, cq_ref, 256)
    gq = _rope(_group_rms(u[:, _C_GQ:_C_GQ + 256], bd, inv64) * gqn_ref[...], cq, slq, shq, 16)
    for hh in range(N_HEADS):
        gq_o[hh] = gq[:, 64 * hh:64 * hh + 64].astype(BF16)
    gk = _rope(_group_rms(u[:, _C_GK:_C_GK + 128], bd[:128, :128], inv64) * gkn_ref[...],
               cq[:, :128], slq[:, :128], shq[:, :128], 16)
    gv = u[:, _C_GV:_C_GV + 128]
    for hh in range(GQA_KV_HEADS):
        gk_o[hh] = gk[:, 64 * hh:64 * hh + 64].astype(BF16)
        gv_o[hh] = jnp.concatenate([gv[:, 64 * hh:64 * hh + 64], ones64], axis=-1).astype(BF16)

    gm = gm_ref[...]
    invn = invn_ref[...]
    cm, slm, shm = _rope_tile(rm_ref, cm_ref, MLA_PAD)
    cqn =_rms(u[:, _C_MCQ:_C_MCQ + 256]) * qag_ref[...]
    mq = jnp.dot(cqn.astype(BF16), wuq_ref[...], preferred_element_type=F32)
    mq = _group_rms(mq, gm, invn) * mqn_ref[...]
    mq = _rope(mq, _tile4(cm), _tile4(slm), _tile4(shm), 8)
    ckvn = (_rms(u[:, _C_MCKV:_C_MCKV + 128]) * kvag_ref[...]).astype(BF16)
    mkn = jnp.dot(ckvn, wukvk_ref[...], preferred_element_type=F32)
    mkn = _group_rms(mkn, gm, invn) * mknn_ref[...]
    mvv = jnp.dot(ckvn, wukvv_ref[...], preferred_element_type=F32)
    kr = _group_rms(u[:, _C_MKR:_C_MKR + 128], gm[:128, :128], invn[:, :128]) * mknr_ref[...]
    kr = _rope(kr, cm, slm, shm, 8)
    for hh in range(N_HEADS):
        mq_o[hh] = mq[:, 128 * hh:128 * hh + 128].astype(BF16)
        mk_o[hh] = (mkn[:, 128 * hh:128 * hh + 128] + kr).astype(BF16)
        mv_o[hh] = jnp.concatenate([mvv[:, 64 * hh:64 * hh + 64], ones64], axis=-1).astype(BF16)

    lx_o[...] = u[:, _C_LX:_C_LX + 256]
    g = u[:, _C_LG:_C_LG + 256]
    lg_o[...] = 0.5 * g * (1.0 + jnp.tanh(0.7978845608028654 * (g + 0.044715 * g * g * g)))


def _premix(x, mod, S, tm, consts, lw):
    T, D = x.shape
    nL = S // tm
    row = lambda w: pl.BlockSpec((tm, w), lambda i: (i, 0))
    hm = lambda hn, w: pl.BlockSpec((hn, tm, w), lambda i: (0, i, 0))
    in_specs = [
        row(D),
        pl.BlockSpec((1, 6, D), lambda i: (jnp.where(i >= nL, 1, 0), 0, 0)),
        _full((1, D)), _full((D, IN_COLS_R)), _full((256, 256)), _full((512, 512)),
        pl.BlockSpec((1, tm // GRID_W, 768), lambda i: (i, 0, 0)),
        pl.BlockSpec((1, tm, 768), lambda i: (jnp.where(i >= nL, 1, 0), 0, 0)),
        pl.BlockSpec((1, tm // GRID_W, 3 * MLA_PAD), lambda i: (i, 0, 0)),
        pl.BlockSpec((1, tm, 3 * MLA_PAD), lambda i: (jnp.where(i >= nL, 1, 0), 0, 0)),
        _full((1, 256)), _full((1, 256)), _full((1, 256)), _full((1, 128)),
        _full((1, 256)), _full((1, 128)),
        _full((1, 512)), _full((1, 512)), _full((1, 128)), _full((1, 512)),
        _full((256, 512)), _full((128, 512)), _full((128, 256)),
    ]
    out_specs = [row(256), row(256), row(256),
                 hm(4, 64), hm(2, 64), hm(2, 128),
                 hm(4, 128), hm(4, 128), hm(4, 128),
                 row(256), row(256)]
    sds = jax.ShapeDtypeStruct
    out_shape = [sds((T, 256), BF16), sds((T, 256), BF16), sds((T, 256), BF16),
                 sds((4, T, 64), BF16), sds((2, T, 64), BF16), sds((2, T, 128), BF16),
                 sds((4, T, 128), BF16), sds((4, T, 128), BF16), sds((4, T, 128), BF16),
                 sds((T, 256), F32), sds((T, 256), F32)]
    return pl.pallas_call(
        _premix_kernel, grid=(T // tm,), in_specs=in_specs, out_specs=out_specs,
        out_shape=out_shape, compiler_params=_params(1), name="premix",
    )(x, mod, lw["g1"], lw["w_in"], consts["bd64"], consts["gm"],
      consts["rope_q_row"], consts["rope_q_col"], consts["rope_m_row"], consts["rope_m_col"],
      lw["na_qn"], lw["na_kn"], lw["gqa_qn"], lw["gqa_kn"], lw["qa_g"], lw["kva_g"],
      lw["mla_qn"], lw["mla_knn"], lw["mla_knr"], consts["invn"],
      lw["wuq"], lw["wukv_k"], lw["wukv_v"])


def _flash_kernel(q_ref, k_ref, v_ref, o_ref, m_scr, acc_scr, *, group, rb):
    j = pl.program_id(1)
    n_kv = k_ref.shape[0]
    tq = q_ref.shape[1]

    @pl.when(j == 0)
    def _():
        m_scr[...] = jnp.full(m_scr.shape, NEG, F32)
        acc_scr[...] = jnp.zeros(acc_scr.shape, F32)

    chains = [(hq, slice(r0, r0 + rb)) for hq in range(n_kv * group) for r0 in range(0, tq, rb)]

    def scores(chain):
        hq, rows = chain
        return lax.dot_general(q_ref[hq, rows, :], k_ref[hq // group], (((1,), (1,)), ((), ())),
                               preferred_element_type=F32)

    s = scores(chains[0])
    for n, (hq, rows) in enumerate(chains):
        s_next = scores(chains[n + 1]) if n + 1 < len(chains) else None
        m_old = m_scr[hq, rows, :]
        m_new = jnp.maximum(m_old, jnp.max(s, axis=-1, keepdims=True))
        alpha = jnp.exp2(m_old - m_new)
        p = jnp.exp2(s - m_new).astype(BF16)
        acc_scr[hq, rows, :] = (alpha * acc_scr[hq, rows, :]
                                + jnp.dot(p, v_ref[hq // group], preferred_element_type=F32))
        m_scr[hq, rows, :] = m_new
        s = s_next

    @pl.when(j == pl.num_programs(1) - 1)
    def _():
        outs = []
        for hq in range(n_kv * group):
            a = acc_scr[hq]
            outs.append(a[:, :HEAD_DIM] / a[:, HEAD_DIM:HEAD_DIM + 1])
        o_ref[...] = jnp.concatenate(outs, axis=-1)


def _flash(q, k, v, tq, tk, n_q, q_blk0, n_kv, kv_blk0):
    hq, _, d = q.shape
    hk = k.shape[0]
    group = hq // hk
    return pl.pallas_call(
        functools.partial(_flash_kernel, group=group, rb=min(FLASH_ROW_BLOCK, tq)),
        grid=(n_q, n_kv),
        in_specs=[pl.BlockSpec((hq, tq, d), lambda i, j: (0, q_blk0 + i, 0)),
                  pl.BlockSpec((hk, tk, d), lambda i, j: (0, kv_blk0 + j, 0)),
                  pl.BlockSpec((hk, tk, 128), lambda i, j: (0, kv_blk0 + j, 0))],
        out_specs=pl.BlockSpec((tq, hq * HEAD_DIM), lambda i, j: (i, 0)),
        out_shape=jax.ShapeDtypeStruct((n_q * tq, hq * HEAD_DIM), F32),
        scratch_shapes=[pltpu.VMEM((hq, tq, 1), F32), pltpu.VMEM((hq, tq, 128), F32)],
        compiler_params=_params(2), name="flash_attn",
    )(q, k, v)


def _na_kernel(q_ref, k_ref, v_ref, tab_ref, o_ref, *, rows, n_ctx):
    i = pl.program_id(0)
    s_lat = rows * GRID_W
    kc = k_ref[pl.ds(s_lat, n_ctx), :]
    vc = v_ref[pl.ds(s_lat, n_ctx), :]
    nq = N_HEADS * GRID_W
    same_head = (lax.broadcasted_iota(jnp.int32, (nq, GROUP_W), 0) // GRID_W
                 == lax.broadcasted_iota(jnp.int32, (nq, GROUP_W), 1) // HEAD_DIM)
    q = q_ref[...].astype(F32)
    q_bd = jnp.where(same_head, jnp.concatenate([q] * N_HEADS, axis=0), 0.0).astype(BF16)
    nt = (((1,), (1,)), ((), ()))
    s_cx = lax.dot_general(q_bd, kc, nt, preferred_element_type=F32)

    def heads_out(res):
        res = jnp.where(same_head, res, 0.0)
        out = res[0:GRID_W]
        for h in range(1, N_HEADS):
            out = out + res[GRID_W * h:GRID_W * (h + 1)]
        return out

    @pl.when(i < rows)
    def _():
        r0 = jnp.clip(i - NA_KH // 2, 0, rows - NA_KH)
        start = pl.multiple_of(r0 * GRID_W, GRID_W)
        ks = k_ref[pl.ds(start, NA_KH * GRID_W), :]
        vs = v_ref[pl.ds(start, NA_KH * GRID_W), :]
        s_nb = (lax.dot_general(q_bd, ks, nt, preferred_element_type=F32)
                + tab_ref[i - r0].reshape(nq, NA_KH * GRID_W))
        m = jnp.maximum(jnp.max(s_nb, axis=-1, keepdims=True), jnp.max(s_cx, axis=-1, keepdims=True))
        p_nb = jnp.exp(s_nb - m)
        p_cx = jnp.exp(s_cx - m)
        l = jnp.sum(p_nb, axis=-1, keepdims=True) + jnp.sum(p_cx, axis=-1, keepdims=True)
        res = (jnp.dot(p_nb.astype(BF16), vs, preferred_element_type=F32)
               + jnp.dot(p_cx.astype(BF16), vc, preferred_element_type=F32))
        o_ref[...] = heads_out(res / l)

    @pl.when(i >= rows)
    def _():
        m = jnp.max(s_cx, axis=-1, keepdims=True)
        p_cx = jnp.exp(s_cx - m)
        l = jnp.sum(p_cx, axis=-1, keepdims=True)
        res = jnp.dot(p_cx.astype(BF16), vc, preferred_element_type=F32)
        o_ref[...] = heads_out(res / l)


def _neighbourhood(q, k, v, tab, S):
    T = q.shape[0]
    rows = S // GRID_W
    n_ctx = T - S
    resident = lambda shape: pl.BlockSpec(shape, lambda i: (0,) * len(shape),
                                          pipeline_mode=pl.Buffered(1))
    return pl.pallas_call(
        functools.partial(_na_kernel, rows=rows, n_ctx=n_ctx),
        grid=(T // GRID_W,),
        in_specs=[pl.BlockSpec((GRID_W, 256), lambda i: (i, 0)),
                  resident((T, 256)), resident((T, 256)),
                  resident((NA_KH, N_HEADS, GRID_W, NA_KH * GRID_W))],
        out_specs=pl.BlockSpec((GRID_W, 256), lambda i: (i, 0)),
        out_shape=jax.ShapeDtypeStruct((T, 256), F32),
        compiler_params=_params(1), name="nbr_attn",
    )(q, k, v, tab)


def _lru_kernel(x_ref, xp_ref, xn_ref, cw_ref, cb_ref, wa_ref, ba_ref, wi_ref, bi_ref,
                lam_ref, y_ref, h_scr, *, reverse, n_lat, n_ctx):
    j = pl.program_id(0)
    tc = x_ref.shape[0]
    n_all = n_lat + n_ctx
    if reverse:
        chunk = jnp.where(j < n_ctx, n_all - 1 - j, n_all - 1 - j)
    else:
        chunk = jnp.where(j < n_ctx, n_lat + j, j - n_ctx)
    has_prev = jnp.logical_and(chunk != 0, chunk != n_lat)
    has_next = jnp.logical_and(chunk != n_lat - 1, chunk != n_all - 1)

    @pl.when(j == 0)
    def _():
        h_scr[...] = jnp.zeros(h_scr.shape, F32)

    xp = jnp.where(has_prev, xp_ref[...], 0.0)
    xn = jnp.where(has_next, xn_ref[...], 0.0)
    xe = jnp.concatenate([xp, x_ref[...], xn], axis=0)
    ne = tc + 16
    cw = cw_ref[...]
    xc = (cw[0:1] * pltpu.roll(xe, 1, 0)[8:8 + tc]
          + cw[1:2] * xe[8:8 + tc]
          + cw[2:3] * pltpu.roll(xe, ne - 1, 0)[8:8 + tc]
          + cw[3:4] * pltpu.roll(xe, ne - 2, 0)[8:8 + tc]
          + cb_ref[...])

    r = jax.nn.sigmoid(jnp.dot(xc, wa_ref[0], preferred_element_type=F32, precision=HIGHEST)
                       + ba_ref[0])
    gi = jax.nn.sigmoid(jnp.dot(xc, wi_ref[0], preferred_element_type=F32, precision=HIGHEST)
                        + bi_ref[0])
    z = -lam_ref[0]
    softplus = jnp.maximum(z, 0.0) + jnp.log1p(jnp.exp(-jnp.abs(z)))
    log_a = -LRU_C * r * softplus
    a = jnp.exp(log_a)
    b = jnp.sqrt(-jnp.tanh(log_a) * (a * a + 1.0)) * (gi * xc)

    rowi = lax.broadcasted_iota(jnp.int32, (tc, 1), 0)
    s = 1
    while s < tc:
        if reverse:
            keep = rowi < tc - s
            a_s = jnp.where(keep, pltpu.roll(a, tc - s, 0), 1.0)
            b_s = jnp.where(keep, pltpu.roll(b, tc - s, 0), 0.0)
        else:
            keep = rowi >= s
            a_s = jnp.where(keep, pltpu.roll(a, s, 0), 1.0)
            b_s = jnp.where(keep, pltpu.roll(b, s, 0), 0.0)
        b = a * b_s + b
        a = a * a_s
        s *= 2
    hcur = b + a * h_scr[...]
    y_ref[...] = hcur
    h_scr[...] = hcur[0:1] if reverse else hcur[tc - 1:tc]


def _lru_scan(lx, lw, S, tc, reverse):
    T = lx.shape[0]
    n_lat, n_ctx = S // tc, (T - S) // tc
    n_all = n_lat + n_ctx
    d = 1 if reverse else 0
    t8 = tc // 8

    def chunk_of(j):
        if reverse:
            return n_all - 1 - j
        return jnp.where(j < n_ctx, n_lat + j, j - n_ctx)

    dspec = lambda shape: pl.BlockSpec((1,) + shape, lambda j: (d,) + (0,) * len(shape))
    return pl.pallas_call(
        functools.partial(_lru_kernel, reverse=reverse, n_lat=n_lat, n_ctx=n_ctx),
        grid=(n_all,),
        in_specs=[pl.BlockSpec((tc, 256), lambda j: (chunk_of(j), 0)),
                  pl.BlockSpec((8, 256), lambda j: (jnp.maximum(chunk_of(j) * t8 - 1, 0), 0)),
                  pl.BlockSpec((8, 256),
                               lambda j: (jnp.minimum((chunk_of(j) + 1) * t8, T // 8 - 1), 0)),
                  _full((4, 256)), _full((1, 256)),
                  dspec((256, 256)), dspec((1, 256)), dspec((256, 256)), dspec((1, 256)),
                  dspec((1, 256))],
        out_specs=pl.BlockSpec((tc, 256), lambda j: (chunk_of(j), 0)),
        out_shape=jax.ShapeDtypeStruct((T, 256), F32),
        scratch_shapes=[pltpu.VMEM((1, 256), F32)],
        compiler_params=_params(1), name="rglru_bwd" if reverse else "rglru_fwd",
    )(lx, lx, lx, lw["conv_w"], lw["conv_b"], lw["wa"], lw["ba"], lw["wi"], lw["bi"], lw["lam"])


def _merge_kernel(x_ref, mod_ref, oa_ref, obl_ref, obc_ref, ocl_ref, occ_ref, yf_ref, yb_ref,
                  lg_ref, gg_ref, wout_ref, g2_ref, rw_ref, rb_ref, lt_ref, ut_ref,
                  x1_o, tok_o, slot_o, gate_o, cnt_o, *, n_lat_tiles):
    i = pl.program_id(0)
    tm = x_ref.shape[0]
    is_ctx = i >= n_lat_tiles

    gg = gg_ref[...]
    od = (yf_ref[...] + yb_ref[...]) * lg_ref[...]
    ob = jnp.where(is_ctx, obc_ref[...], obl_ref[...])
    oc = jnp.where(is_ctx, occ_ref[...], ocl_ref[...])
    parts = [oa_ref[...], ob, oc, od]
    ycat = jnp.concatenate(
        [(_rms(p) * gg[:, GROUP_W * n:GROUP_W * (n + 1)]).astype(BF16) for n, p in enumerate(parts)],
        axis=-1)
    y = jnp.dot(ycat, wout_ref[...], preferred_element_type=F32)
    gt1 = mod_ref[0, 2:3, :]
    sh2 = mod_ref[0, 3:4, :]
    sc2 = mod_ref[0, 4:5, :]
    x1 = x_ref[...] + gt1 * y
    x1_o[...] = x1
    tok = _rms(x1) * g2_ref[...] * (1.0 + sc2) + sh2
    tok_o[...] = tok.astype(BF16)

    logits = jnp.dot(tok, rw_ref[...], preferred_element_type=F32, precision=HIGHEST) + rb_ref[...]
    lane = lax.broadcasted_iota(jnp.int32, (tm, LANES), 1)
    v_acc = jnp.zeros((tm, LANES), F32)
    onehots = []
    v0 = None
    work = logits
    for k in range(TOP_K):
        vk = jnp.max(work, axis=-1, keepdims=True)
        ek = jnp.min(jnp.where(work == vk, lane, LANES), axis=-1, keepdims=True)
        sel = lane == ek
        onehots.append(sel)
        work = jnp.where(sel, NEG * 2.0, work)
        if k == 0:
            v0 = vk
        v_acc = jnp.where(lane == k, jnp.exp(vk - v0), v_acc)
    gate_o[...] = v_acc / jnp.sum(v_acc, axis=-1, keepdims=True)

    oh = [jnp.where(o, 1.0, 0.0) for o in onehots]
    oh_all = oh[0] + oh[1] + oh[2] + oh[3]
    cnt = jnp.sum(oh_all, axis=0, keepdims=True)
    run = jnp.floor((cnt + (RUN_ALIGN - 1.0)) * (1.0 / RUN_ALIGN)) * RUN_ALIGN
    run_start = jnp.dot(jnp.broadcast_to(run, (8, LANES)).astype(BF16), ut_ref[...],
                        preferred_element_type=F32)[0:1]
    excl = jnp.dot(lt_ref[...], oh_all.astype(BF16), preferred_element_type=F32) + run_start
    slot = jnp.full((tm, LANES), -1.0, F32)
    for k in range(TOP_K):
        sk = jnp.sum(oh[k] * excl, axis=-1, keepdims=True)
        slot = jnp.where(lane == k, sk, slot)
    slot_o[...] = slot.astype(jnp.int32)
    cnt_o[0] = jnp.broadcast_to(cnt, (8, LANES))


def _merge(x, mod, oa, ob_lat, ob_ctx, oc_lat, oc_ctx, yf, yb, lg, S, tm, consts, lw):
    T, D = x.shape
    nL = S // tm
    nt = T // tm
    row = lambda w: pl.BlockSpec((tm, w), lambda i: (i, 0))
    lat = pl.BlockSpec((tm, 256), lambda i: (jnp.minimum(i, nL - 1), 0))
    ctx = pl.BlockSpec((tm, 256), lambda i: (jnp.maximum(i - nL, 0), 0))
    sds = jax.ShapeDtypeStruct
    return pl.pallas_call(
        functools.partial(_merge_kernel, n_lat_tiles=nL), grid=(nt,),
        in_specs=[row(D), pl.BlockSpec((1, 6, D), lambda i: (jnp.where(i >= nL, 1, 0), 0, 0)),
                  row(256), lat, ctx, lat, ctx, row(256), row(256), row(256),
                  _full((1, D)), _full((D, D)), _full((1, D)), _full((D, LANES)),
                  _full((1, LANES)), _full((tm, tm)), _full((LANES, LANES))],
        out_specs=[row(D), row(D), row(LANES), row(LANES),
                   pl.BlockSpec((1, 8, LANES), lambda i: (i, 0, 0))],
        out_shape=[sds((T, D), F32), sds((T, D), BF16), sds((T, LANES), jnp.int32),
                   sds((T, LANES), F32), sds((nt, 8, LANES), F32)],
        compiler_params=_params(1), name="merge_router",
    )(x, mod, oa, ob_lat, ob_ctx, oc_lat, oc_ctx, yf, yb, lg, lw["grp_g"], lw["w_out"], lw["g2"],
      lw["router_w"], lw["router_b"], consts["lt"], consts["ut"])


def _routing_meta(cnt, bm, n_rows):
    c = cnt[:, 0, :N_EXPERTS].astype(jnp.int32)
    nt = c.shape[0]
    run = (c + RUN_ALIGN - 1) // RUN_ALIGN * RUN_ALIGN
    tile_off = jnp.cumsum(run, axis=0) - run
    tot = jnp.sum(run, axis=0)
    padded = (tot + bm - 1) // bm * bm
    seg_end = jnp.cumsum(padded)
    seg_start = seg_end - padded
    g_start = seg_start[None, :] + tile_off
    l_start = jnp.cumsum(run, axis=1) - run
    per_size, done = [], jnp.zeros_like(run)
    for size in CHUNK_ROWS:
        n = (run - done) // size
        per_size.append(jnp.sum(n, axis=1, keepdims=True))
        done = done + n * size
    n_tot = jnp.sum(run, axis=1, keepdims=True) // RUN_ALIGN
    meta = jnp.concatenate(
        [g_start, l_start, run] + per_size
        + [n_tot, jnp.zeros((nt, LANES - 3 * N_EXPERTS - len(CHUNK_ROWS) - 1), jnp.int32)], axis=1)
    n_zero = (padded - tot) // RUN_ALIGN
    zmeta = jnp.concatenate([seg_start + tot, n_zero, jnp.sum(n_zero)[None], seg_end[-1:] // bm,
                             jnp.zeros((LANES - 2 * N_EXPERTS - 2,), jnp.int32)])
    blk_row = jnp.arange(n_rows // bm, dtype=jnp.int32) * bm
    blk_expert = jnp.minimum(jnp.sum((seg_end[None, :] <= blk_row[:, None]).astype(jnp.int32), axis=1),
                             N_EXPERTS - 1)
    n_used = seg_end[-1:] // bm
    return (meta.reshape(nt, 1, LANES).astype(jnp.int32), zmeta.reshape(1, 1, LANES).astype(jnp.int32),
            blk_expert.astype(jnp.int32), n_used.astype(jnp.int32))


def _run_copies(meta_ref, make_copy):
    def per_expert(e, c):
        g = meta_ref[0, 0, e]
        l = meta_ref[0, 0, N_EXPERTS + e]
        rows = meta_ref[0, 0, 2 * N_EXPERTS + e]
        big = CHUNK_ROWS[0]

        def issue(cc, c2):
            off = big * cc
            make_copy(pl.multiple_of(g + off, RUN_ALIGN), pl.multiple_of(l + off, RUN_ALIGN),
                      big).start()
            return c2

        c = lax.fori_loop(0, rows // big, issue, c)
        done = rows // big * big
        for size in CHUNK_ROWS[1:]:
            use = (rows - done) >= size

            @pl.when(use)
            def _(size=size, done=done):
                make_copy(pl.multiple_of(g + done, RUN_ALIGN), pl.multiple_of(l + done, RUN_ALIGN),
                          size).start()

            done = done + jnp.where(use, size, 0)
        return c

    lax.fori_loop(0, N_EXPERTS, per_expert, 0)


def _drain(meta_ref, make_copy):
    for n, size in enumerate(CHUNK_ROWS):
        def body(_, c, size=size):
            make_copy(0, 0, size).wait()
            return c

        lax.fori_loop(0, meta_ref[0, 0, 3 * N_EXPERTS + n], body, 0)


def _dispatch_kernel(meta_ref, prev_meta_ref, zmeta_ref, slot_ref, tok_ref, xin_ref,
                     xs_scr, z_scr, sem):
    i = pl.program_id(0)
    tm = tok_ref.shape[0]
    n_slot = xs_scr.shape[1]
    buf = lax.rem(i, 2)
    slot_t = jnp.transpose(slot_ref[...].astype(F32))
    srow = lax.broadcasted_iota(jnp.int32, (n_slot, tm), 0).astype(F32)
    perm = jnp.zeros((n_slot, tm), F32)
    for k in range(TOP_K):
        perm = perm + jnp.where(srow == slot_t[k:k + 1, :], 1.0, 0.0)
    xs_scr[buf] = jnp.dot(perm.astype(BF16), tok_ref[...], preferred_element_type=F32)

    def to_global(g_row, l_row, size):
        return pltpu.make_async_copy(xs_scr.at[buf, pl.ds(l_row, size)],
                                     xin_ref.at[pl.ds(g_row, size)], sem)

    @pl.when(i > 0)
    def _():
        _drain(prev_meta_ref, to_global)

    _run_copies(meta_ref, to_global)

    @pl.when(i == pl.num_programs(0) - 1)
    def _():
        _drain(meta_ref, to_global)
        z_scr[...] = jnp.zeros(z_scr.shape, F32)

        def zero_copy(g_row):
            return pltpu.make_async_copy(z_scr.at[pl.ds(0, RUN_ALIGN)],
                                         xin_ref.at[pl.ds(g_row, RUN_ALIGN)], sem)

        def per_expert(e, c):
            g = zmeta_ref[0, 0, e]

            def issue(cc, c2):
                zero_copy(pl.multiple_of(g + RUN_ALIGN * cc, RUN_ALIGN)).start()
                return c2

            return lax.fori_loop(0, zmeta_ref[0, 0, N_EXPERTS + e], issue, c)

        lax.fori_loop(0, N_EXPERTS, per_expert, 0)

        def drain(_, c):
            zero_copy(0).wait()
            return c

        lax.fori_loop(0, zmeta_ref[0, 0, 2 * N_EXPERTS], drain, 0)

        bm = z_scr.shape[0]

        def block_copy(b):
            return pltpu.make_async_copy(z_scr, xin_ref.at[pl.ds(pl.multiple_of(b * bm, bm), bm)], sem)

        def issue_block(b, c):
            block_copy(b).start()
            return c

        def drain_block(b, c):
            block_copy(b).wait()
            return c

        n_used = zmeta_ref[0, 0, 2 * N_EXPERTS + 1]
        lax.fori_loop(n_used, xin_ref.shape[0] // bm, issue_block, 0)
        lax.fori_loop(n_used, xin_ref.shape[0] // bm, drain_block, 0)


def _dispatch(tok, slot, meta, zmeta, n_rows, tm, bm):
    T, D = tok.shape
    n_slot = TOP_K * tm + N_EXPERTS * RUN_ALIGN
    smem = lambda idx: pl.BlockSpec((1, 1, LANES), idx, memory_space=pltpu.SMEM)
    return pl.pallas_call(
        _dispatch_kernel,
        grid=(T // tm,),
        in_specs=[smem(lambda i: (i, 0, 0)), smem(lambda i: (jnp.maximum(i - 1, 0), 0, 0)),
                  smem(lambda i: (0, 0, 0)),
                  pl.BlockSpec((tm, LANES), lambda i: (i, 0)),
                  pl.BlockSpec((tm, D), lambda i: (i, 0))],
        out_specs=pl.BlockSpec(memory_space=pl.ANY),
        out_shape=jax.ShapeDtypeStruct((n_rows, D), F32),
        scratch_shapes=[pltpu.VMEM((2, n_slot, D), F32), pltpu.VMEM((bm, D), F32),
                        pltpu.SemaphoreType.DMA(())],
        compiler_params=_params(1), name="moe_dispatch",
    )(meta, meta, zmeta, slot, tok)


def _expert_kernel(be_ref, nu_ref, x_ref, wgu_ref, bgu_ref, wd_ref, bd_ref, y_ref,
                   wgu_bf, wd_bf):
    i = pl.program_id(0)
    d_e = wd_ref.shape[1]

    @pl.when(i < nu_ref[0])
    def _():
        @pl.when(jnp.logical_or(i == 0, be_ref[i] != be_ref[jnp.maximum(i - 1, 0)]))
        def _():
            wgu_bf[...] = wgu_ref[0].astype(BF16)
            wd_bf[...] = wd_ref[0].astype(BF16)

        gu = jnp.dot(x_ref[...].astype(BF16), wgu_bf[...], preferred_element_type=F32) + bgu_ref[0]
        x_glu = jnp.minimum(gu[:, :d_e], SWIGLU_LIMIT)
        x_lin = jnp.clip(gu[:, d_e:], -SWIGLU_LIMIT, SWIGLU_LIMIT)
        act = x_glu * jax.nn.sigmoid(SWIGLU_ALPHA * x_glu) * (x_lin + 1.0)
        y_ref[...] = jnp.dot(act.astype(BF16), wd_bf[...], preferred_element_type=F32) + bd_ref[0]

    @pl.when(i >= nu_ref[0])
    def _():
        y_ref[...] = jnp.zeros(y_ref.shape, F32)


def _experts(xin, blk_expert, n_used, layer, w_gu, b_gu, w_down, b_down, bm):
    P, D = xin.shape
    d2 = w_gu.shape[-1]
    d_e = d2 // 2
    blk = lambda i, be, nu: jnp.minimum(i, nu[0] - 1)
    wspec = lambda r, c: pl.BlockSpec((None, 1, r, c),
                                      lambda i, be, nu: (layer, be[blk(i, be, nu)], 0, 0))
    grid_spec = pltpu.PrefetchScalarGridSpec(
        num_scalar_prefetch=2, grid=(P // bm,),
        in_specs=[pl.BlockSpec((bm, D), lambda i, be, nu: (blk(i, be, nu), 0)),
                  wspec(D, d2), wspec(1, d2), wspec(d_e, D), wspec(1, D)],
        out_specs=pl.BlockSpec((bm, D), lambda i, be, nu: (i, 0)),
        scratch_shapes=[pltpu.VMEM((D, d2), BF16), pltpu.VMEM((d_e, D), BF16)])
    return pl.pallas_call(
        _expert_kernel, grid_spec=grid_spec,
        out_shape=jax.ShapeDtypeStruct((P, D), F32),
        compiler_params=_params(1), name="moe_experts",
    )(blk_expert, n_used, xin, w_gu, b_gu, w_down, b_down)


def _combine_kernel(meta_ref, next_meta_ref, slot_ref, gate_ref, y_ref, x_ref, mod_ref, o_ref,
                    ybuf, sem):
    i = pl.program_id(0)
    tm = x_ref.shape[0]
    _, n_slot, d = ybuf.shape
    buf = lax.rem(i, 2)

    def fetch(m_ref, b):
        def to_local(g_row, l_row, size):
            return pltpu.make_async_copy(y_ref.at[pl.ds(g_row, size)],
                                         ybuf.at[b, pl.ds(l_row, size)], sem)

        _run_copies(m_ref, to_local)
        n_tot = m_ref[0, 0, 3 * N_EXPERTS + len(CHUNK_ROWS)]

        def zero_tail(cc, c):
            ybuf[b, pl.ds(pl.multiple_of(RUN_ALIGN * cc, RUN_ALIGN), RUN_ALIGN), :] = jnp.zeros(
                (RUN_ALIGN, d), F32)
            return c

        lax.fori_loop(n_tot, n_slot // RUN_ALIGN, zero_tail, 0)
        return to_local

    @pl.when(i == 0)
    def _():
        fetch(meta_ref, buf)

    def wait_shape(g_row, l_row, size):
        return pltpu.make_async_copy(y_ref.at[pl.ds(g_row, size)],
                                     ybuf.at[buf, pl.ds(l_row, size)], sem)

    _drain(meta_ref, wait_shape)

    @pl.when(i + 1 < pl.num_programs(0))
    def _():
        fetch(next_meta_ref, 1 - buf)

    slot = slot_ref[...]
    gate = gate_ref[...]
    col = lax.broadcasted_iota(jnp.int32, (tm, n_slot), 1)
    w = jnp.zeros((tm, n_slot), F32)
    for k in range(TOP_K):
        w = w + jnp.where(col == slot[:, k:k + 1], gate[:, k:k + 1], 0.0)
    w_hi = w.astype(BF16)
    w_lo = (w - w_hi.astype(F32)).astype(BF16)
    yb = ybuf[buf].astype(BF16)
    f = (jnp.dot(w_hi, yb, preferred_element_type=F32)
         + jnp.dot(w_lo, yb, preferred_element_type=F32))
    o_ref[...] = x_ref[...] + mod_ref[0, 5:6, :] * f


def _combine(y, slot, gate, meta, x1, mod, S, tm):
    T, D = x1.shape
    nL = S // tm
    n_slot = TOP_K * tm + N_EXPERTS * RUN_ALIGN
    return pl.pallas_call(
        _combine_kernel,
        grid=(T // tm,),
        in_specs=[pl.BlockSpec((1, 1, LANES), lambda i: (i, 0, 0), memory_space=pltpu.SMEM),
                  pl.BlockSpec((1, 1, LANES), lambda i: (jnp.minimum(i + 1, T // tm - 1), 0, 0),
                               memory_space=pltpu.SMEM),
                  pl.BlockSpec((tm, LANES), lambda i: (i, 0)),
                  pl.BlockSpec((tm, LANES), lambda i: (i, 0)),
                  pl.BlockSpec(memory_space=pl.ANY),
                  pl.BlockSpec((tm, D), lambda i: (i, 0)),
                  pl.BlockSpec((1, 6, D), lambda i: (jnp.where(i >= nL, 1, 0), 0, 0))],
        out_specs=pl.BlockSpec((tm, D), lambda i: (i, 0)),
        out_shape=jax.ShapeDtypeStruct((T, D), F32),
        scratch_shapes=[pltpu.VMEM((2, n_slot, D), F32), pltpu.SemaphoreType.DMA(())],
        compiler_params=_params(1), name="moe_combine",
    )(meta, meta, slot, gate, y, x1, mod)


def _rope_tables(S, T, tm, rot_dim, width, lane0, reps):
    h = rot_dim // 4
    ax = rot_dim // 2
    inv = ROPE_THETA ** (-jnp.arange(0, ax, 2, dtype=F32) / ax)

    def parts(n):
        ang = jnp.arange(n, dtype=F32)[:, None] * inv
        return jnp.cos(ang), jnp.sin(ang), jnp.zeros((n, h), F32)

    def place(blocks, fill):
        n = blocks[0].shape[0]
        tab = jnp.concatenate(
            [jnp.full((n, lane0), fill, F32)] + list(blocks)
            + [jnp.full((n, width - lane0 - rot_dim), fill, F32)], axis=-1)
        return jnp.tile(tab, (1, reps))

    n_rows = S // GRID_W
    c, s, z = parts(n_rows)
    row = jnp.concatenate([place([c, c, z, z], 0.0), place([z, s, z, z], 0.0),
                           place([-s, z, z, z], 0.0)], axis=-1)
    row = jnp.concatenate([row, jnp.zeros(((T - S) // GRID_W, row.shape[1]), F32)], axis=0)
    c, s, z = parts(GRID_W)
    col = jnp.concatenate([place([z, z, c, c], 1.0), place([z, z, z, s], 0.0),
                           place([z, z, -s, z], 0.0)], axis=-1)
    col = jnp.tile(col, (tm // GRID_W, 1))
    ident = jnp.concatenate([jnp.ones((tm, reps * width), F32),
                             jnp.zeros((tm, 2 * reps * width), F32)], axis=-1)
    return row.reshape(T // tm, tm // GRID_W, -1), jnp.stack([col, ident])


def _constants(S, T, tm):
    rope_q_row, rope_q_col = _rope_tables(S, T, tm, HEAD_DIM, HEAD_DIM, 0, N_HEADS)
    rope_m_row, rope_m_col = _rope_tables(S, T, tm, MLA_ROPE, MLA_PAD, MLA_NOPE, 1)
    bd64 = np.kron(np.eye(4, dtype=np.float32), np.ones((64, 64), np.float32))
    sizes = [MLA_NOPE, MLA_ROPE, MLA_PAD - MLA_NOPE - MLA_ROPE] * N_HEADS
    gid = np.repeat(np.arange(len(sizes)), sizes)
    gm = (gid[:, None] == gid[None, :]).astype(np.float32)
    invn = (1.0 / np.repeat(np.asarray(sizes, np.float32), sizes))[None, :]
    lt = np.tril(np.ones((tm, tm), np.float32), -1)
    ut = np.triu(np.ones((LANES, LANES), np.float32), 1)
    return dict(rope_q_row=rope_q_row, rope_q_col=rope_q_col,
                rope_m_row=rope_m_row, rope_m_col=rope_m_col,
                bd64=jnp.asarray(bd64, BF16), gm=jnp.asarray(gm, BF16),
                invn=jnp.asarray(invn), lt=jnp.asarray(lt, BF16), ut=jnp.asarray(ut, BF16))


def _na_bias_tables(rpb):
    qc = np.arange(GRID_W)[:, None]
    kc = np.arange(GRID_W)[None, :]
    c0 = np.clip(qc - NA_KW // 2, 0, GRID_W - NA_KW)
    valid = (kc >= c0) & (kc < c0 + NA_KW)
    sel = ((kc - qc + NA_KW - 1)[:, :, None] == np.arange(2 * NA_KW - 1)) & valid[:, :, None]
    toep = jnp.einsum("lhab,qkb->lhaqk", rpb, jnp.asarray(sel, F32), precision=HIGHEST)
    toep = jnp.where(jnp.asarray(valid)[None, None, None], toep, NEG)
    L, H = rpb.shape[:2]
    tabs = []
    for off in range(NA_KH):
        rows = toep[:, :, NA_KH - 1 - off:2 * NA_KH - 1 - off]
        tabs.append(rows.transpose(0, 1, 3, 2, 4).reshape(L, H, GRID_W, NA_KH * GRID_W))
    return jnp.stack(tabs, axis=1)


def _block_diag(w):
    n, bw, _ = w.shape
    eye = jnp.eye(n, dtype=w.dtype)
    return (eye[:, None, :, None] * w[:, :, None, :]).reshape(n * bw, n * bw)


def _layer_weights(l, p):
    D = p["w_in"].shape[1]
    w_in = p["w_in"][l]
    offs = np.cumsum([0, 256, 256, 256, 256, 128, 128, 256, 128, 32, 256, 256])
    seg = lambda n: w_in[:, offs[n]:offs[n + 1]]
    z = lambda n: jnp.zeros((D, n), F32)
    w_in_r = jnp.concatenate(
        [seg(0), seg(1), seg(2), seg(3), seg(4), seg(5), seg(6), seg(7), seg(9), seg(10),
         z(MLA_NOPE), seg(8), z(MLA_PAD - MLA_NOPE - MLA_ROPE)], axis=-1).astype(BF16)
    sc = HEAD_DIM ** -0.5
    sc_m = (MLA_NOPE + MLA_ROPE) ** -0.5
    t4 = lambda g: jnp.tile(g, N_HEADS)[None, :]
    wuq = p["mla_wuq"][l].reshape(-1, N_HEADS, MLA_NOPE + MLA_ROPE)
    wuq = jnp.pad(wuq, ((0, 0), (0, 0), (0, MLA_PAD - MLA_NOPE - MLA_ROPE)))
    wukv = p["mla_wukv"][l].reshape(-1, N_HEADS, MLA_NOPE + HEAD_DIM)
    wukv_k = jnp.pad(wukv[:, :, :MLA_NOPE], ((0, 0), (0, 0), (0, MLA_PAD - MLA_NOPE)))
    qn, kn = p["mla_qn"][l], p["mla_kn"][l]
    padq = jnp.pad(qn * (sc_m * LOG2E), (0, MLA_PAD - MLA_NOPE - MLA_ROPE))
    padkn = jnp.pad(kn[:MLA_NOPE], (0, MLA_PAD - MLA_NOPE))
    padkr = jnp.pad(kn[MLA_NOPE:], (MLA_NOPE, MLA_PAD - MLA_NOPE - MLA_ROPE))
    rw =jnp.pad(p["router_w"][l], ((0, 0), (0, LANES - N_EXPERTS)))
    rb = jnp.pad(p["router_b"][l], (0, LANES - N_EXPERTS), constant_values=NEG)[None, :]
    return dict(
        g1=p["norm1_g"][l][None, :], g2=p["norm2_g"][l][None, :], w_in=w_in_r,
        na_qn=t4(p["na_qn"][l] * sc), na_kn=t4(p["na_kn"][l]),
        gqa_qn=t4(p["gqa_qn"][l] * (sc * LOG2E)), gqa_kn=jnp.tile(p["gqa_kn"][l], GQA_KV_HEADS)[None, :],
        qa_g=p["mla_qa_g"][l][None, :], kva_g=p["mla_kva_g"][l][None, :],
        mla_qn=t4(padq), mla_knn=t4(padkn), mla_knr=padkr[None, :],
        wuq=wuq.reshape(-1, N_HEADS * MLA_PAD).astype(BF16),
        wukv_k=wukv_k.reshape(-1, N_HEADS * MLA_PAD).astype(BF16),
        wukv_v=wukv[:, :, MLA_NOPE:].reshape(-1, N_HEADS * HEAD_DIM).astype(BF16),
        conv_w=p["lru_conv_w"][l][:, 0, :], conv_b=p["lru_conv_b"][l][None, :],
        wa=jnp.stack([_block_diag(p["lru_wa"][l][d]) for d in range(2)]),
        wi=jnp.stack([_block_diag(p["lru_wi"][l][d]) for d in range(2)]),
        ba=p["lru_ba"][l][:, None, :], bi=p["lru_bi"][l][:, None, :],
        lam=p["lru_lam"][l][:, None, :],
        grp_g=p["grp_g"][l][None, :], w_out=p["w_out"][l].astype(BF16),
        router_w=rw, router_b=rb,
    )


def _tiles(S, C):
    tm = min(256, C)
    tq = min(1024, S)
    T = S + C
    tk = next(t for t in (3328, 1280, 640, 256, 128) if T % t == 0)
    return dict(tm=tm, tq=tq, tk=tk, tc=min(256, C), bm=512)


def kernel(x, c, ctx, c_ctx, ada_w, ada_b, norm1_g, norm2_g, w_in, na_qn, na_kn, na_rpb, gqa_qn, gqa_kn, mla_qa_g, mla_kva_g, mla_wuq, mla_wukv, mla_qn, mla_kn, lru_conv_w, lru_conv_b, lru_wa, lru_ba, lru_wi, lru_bi, lru_lam, grp_g, w_out, router_w, router_b, exp_w_gu, exp_b_gu, exp_w_down, exp_b_down):
    p = dict(norm1_g=norm1_g, norm2_g=norm2_g, w_in=w_in, na_qn=na_qn, na_kn=na_kn, na_rpb=na_rpb,
             gqa_qn=gqa_qn, gqa_kn=gqa_kn, mla_qa_g=mla_qa_g, mla_kva_g=mla_kva_g,
             mla_wuq=mla_wuq, mla_wukv=mla_wukv, mla_qn=mla_qn, mla_kn=mla_kn,
             lru_conv_w=lru_conv_w, lru_conv_b=lru_conv_b, lru_wa=lru_wa, lru_ba=lru_ba,
             lru_wi=lru_wi, lru_bi=lru_bi, lru_lam=lru_lam, grp_g=grp_g, w_out=w_out,
             router_w=router_w, router_b=router_b, exp_w_gu=exp_w_gu, exp_b_gu=exp_b_gu,
             exp_w_down=exp_w_down, exp_b_down=exp_b_down)
    B, S, D = x.shape
    assert B == 1 and S % GRID_W == 0
    C = ctx.shape[1]
    T = S + C
    L = ada_w.shape[0]
    ts = _tiles(S, C)
    tm, bm = ts["tm"], ts["bm"]
    consts = _constants(S, T, tm)

    c8 = jnp.zeros((8, D), F32).at[0].set(c[0]).at[1].set(c_ctx)
    mods = _modulation(c8, ada_w, ada_b)[:, :2].reshape(L, 2, 6, D)

    n_rows = -(-(T * TOP_K + (T // tm) * N_EXPERTS * (RUN_ALIGN - 1) + N_EXPERTS * (bm - 1)) // bm) * bm
    na_tabs = _na_bias_tables(na_rpb)
    n_exp = exp_w_gu.shape[1]
    w_gu, w_down = exp_w_gu, exp_w_down
    b_gu, b_down = exp_b_gu.reshape(L, n_exp, 1, -1), exp_b_down.reshape(L, n_exp, 1, -1)
    tq, tk = ts["tq"], ts["tk"]
    xs = jnp.concatenate([x[0], ctx[0]], axis=0)
    for l in range(L):
        lw = _layer_weights(l, p)
        mod = mods[l]
        (naq, nak, nav, gq, gk, gv, mq, mk, mv, lx, lg) = _premix(xs, mod, S, tm, consts, lw)
        oa = _neighbourhood(naq, nak, nav, na_tabs[l], S)
        ob_lat = _flash(gq, gk, gv, tq, tk, S // tq, 0, T // tk, 0)
        ob_ctx = _flash(gq, gk, gv, C, C, 1, S // C, 1, S // C)
        oc_lat = _flash(mq, mk, mv, tq, tk, S // tq, 0, T // tk, 0)
        oc_ctx = _flash(mq, mk, mv, C, C, 1, S // C, 1, S // C)
        yf = _lru_scan(lx, lw, S, ts["tc"], False)
        yb = _lru_scan(lx, lw, S, ts["tc"], True)
        x1, tok, slot, gate, cnt = _merge(xs, mod, oa, ob_lat, ob_ctx, oc_lat, oc_ctx, yf, yb, lg,
                                          S, tm, consts, lw)
        meta, zmeta, blk_expert, n_used = _routing_meta(cnt, bm, n_rows)
        xin = _dispatch(tok, slot, meta, zmeta, n_rows, tm, bm)
        y = _experts(xin, blk_expert, n_used, l, w_gu, b_gu, w_down, b_down, bm)
        xs = _combine(y, slot, gate, meta, x1, mod, S, tm)
    return xs[:S][None]
```

```python
import functools

import numpy as np
import jax
import jax.numpy as jnp
from jax import lax
from jax.experimental import pallas as pl
from jax.experimental.pallas import tpu as pltpu

F32 = jnp.float32
BF16 = jnp.bfloat16
HIGHEST = lax.Precision.HIGHEST

GRID_W = 64
HEAD_DIM = 64
N_HEADS = 4
GROUP_W = 256
GQA_KV_HEADS = 2
NA_KH = 8
NA_KW = 16
MLA_NOPE = 64
MLA_ROPE = 32
MLA_PAD = 128
LRU_C = 8.0
N_EXPERTS = 32
TOP_K = 4
SWIGLU_LIMIT = 7.0
SWIGLU_ALPHA = 1.702
ROPE_THETA = 10000.0
EPS = 1e-6
NEG = -1e30
LANES = 128
FLASH_ROW_BLOCK = 512
LOG2E = 1.4426950408889634
RUN_ALIGN = 8
CHUNK_ROWS = (32, 16, 8)
assert all(a == 2 * b for a, b in zip(CHUNK_ROWS, CHUNK_ROWS[1:])) and CHUNK_ROWS[-1] == RUN_ALIGN
VMEM_LIMIT = 56 * 1024 * 1024

_C_NAQ, _C_NAK, _C_NAV = 0, 256, 512
_C_GQ, _C_GK, _C_GV = 768, 1024, 1152
_C_MCQ, _C_MCKV = 1280, 1536
_C_LX, _C_LG = 1664, 1920
_C_MKR = 2176
IN_COLS_R = 2304


def _params(n_axes, vmem=VMEM_LIMIT):
    return pltpu.CompilerParams(dimension_semantics=("arbitrary",) * n_axes,
                                vmem_limit_bytes=vmem)


def _full(shape):
    n = len(shape)
    return pl.BlockSpec(shape, lambda *_: (0,) * n)


def _mod_kernel(c_ref, w_ref, b_ref, o_ref):
    cc = c_ref[...]
    s = cc * jax.nn.sigmoid(cc)
    o_ref[0] = jnp.dot(s, w_ref[0], preferred_element_type=F32, precision=HIGHEST) + b_ref[0]


def _modulation(c8, ada_w, ada_b):
    L, D, D6 = ada_w.shape
    tn = D6 // 6
    return pl.pallas_call(
        _mod_kernel,
        grid=(L, D6 // tn),
        in_specs=[pl.BlockSpec((8, D), lambda l, j: (0, 0)),
                  pl.BlockSpec((1, D, tn), lambda l, j: (l, 0, j)),
                  pl.BlockSpec((1, 1, tn), lambda l, j: (l, 0, j))],
        out_specs=pl.BlockSpec((1, 8, tn), lambda l, j: (l, 0, j)),
        out_shape=jax.ShapeDtypeStruct((L, 8, D6), F32),
        compiler_params=_params(2),
        name="adaln_mod",
    )(c8, ada_w, ada_b.reshape(L, 1, D6))


def _rms(z):
    return z * lax.rsqrt(jnp.mean(z * z, axis=-1, keepdims=True) + EPS)


def _group_rms(z, gmat, inv_n):
    zz = z * z
    hi = zz.astype(BF16)
    lo = (zz - hi.astype(F32)).astype(BF16)
    ss = (jnp.dot(hi, gmat, preferred_element_type=F32)
          + jnp.dot(lo, gmat, preferred_element_type=F32))
    return z * lax.rsqrt(ss * inv_n + EPS)


def _rope(z, cos, sin_lo, sin_hi, half):
    w = z.shape[-1]
    return z * cos + pltpu.roll(z, half, 1) * sin_lo + pltpu.roll(z, w - half, 1) * sin_hi


def _tile4(t):
    return jnp.concatenate([t, t, t, t], axis=-1)


def _rope_tile(row_ref, col_ref, width):
    rt = row_ref[0]
    rows = jnp.concatenate([jnp.broadcast_to(rt[g:g + 1], (GRID_W, rt.shape[1]))
                            for g in range(rt.shape[0])], axis=0)
    tab = rows + col_ref[0]
    return tab[:, :width], tab[:, width:2 * width], tab[:, 2 * width:]


def _premix_kernel(x_ref, mod_ref, g1_ref, win_ref, bd_ref, gm_ref,
                   rq_ref, cq_ref, rm_ref, cm_ref,
                   naqn_ref, nakn_ref, gqn_ref, gkn_ref, qag_ref, kvag_ref,
                   mqn_ref, mknn_ref, mknr_ref, invn_ref, wuq_ref, wukvk_ref, wukvv_ref,
                   naq_o, nak_o, nav_o, gq_o, gk_o, gv_o, mq_o, mk_o, mv_o, lx_o, lg_o):
    tm = x_ref.shape[0]
    xt = x_ref[...]
    sh1 = mod_ref[0, 0:1, :]
    sc1 = mod_ref[0, 1:2, :]
    h = _rms(xt) * g1_ref[...] * (1.0 + sc1) + sh1
    u = jnp.dot(h.astype(BF16), win_ref[...], preferred_element_type=F32)

    bd = bd_ref[...]
    inv64 = 1.0 / HEAD_DIM
    ones64 = jnp.ones((tm, HEAD_DIM), F32)

    naq_o[...] = (_group_rms(u[:, _C_NAQ:_C_NAQ + 256], bd, inv64) * naqn_ref[...]).astype(BF16)
    nak_o[...] = (_group_rms(u[:, _C_NAK:_C_NAK + 256], bd, inv64) * nakn_ref[...]).astype(BF16)
    nav_o[...] = u[:, _C_NAV:_C_NAV + 256].astype(BF16)

    cq, slq, shq = _rope_tile(rq_ref, cq_ref, 256)
    gq = _rope(_group_rms(u[:, _C_GQ:_C_GQ + 256], bd, inv64) * gqn_ref[...], cq, slq, shq, 16)
    for hh in range(N_HEADS):
        gq_o[hh] = gq[:, 64 * hh:64 * hh + 64].astype(BF16)
    gk = _rope(_group_rms(u[:, _C_GK:_C_GK + 128], bd[:128, :128], inv64) * gkn_ref[...],
               cq[:, :128], slq[:, :128], shq[:, :128], 16)
    gv = u[:, _C_GV:_C_GV + 128]
    for hh in range(GQA_KV_HEADS):
        gk_o[hh] = gk[:, 64 * hh:64 * hh + 64].astype(BF16)
        gv_o[hh] = jnp.concatenate([gv[:, 64 * hh:64 * hh + 64], ones64], axis=-1).astype(BF16)

    gm = gm_ref[...]
    invn = invn_ref[...]
    cm, slm, shm = _rope_tile(rm_ref, cm_ref, MLA_PAD)
    cqn =_rms(u[:, _C_MCQ:_C_MCQ + 256]) * qag_ref[...]
    mq = jnp.dot(cqn.astype(BF16), wuq_ref[...], preferred_element_type=F32)
    mq = _group_rms(mq, gm, invn) * mqn_ref[...]
    mq = _rope(mq, _tile4(cm), _tile4(slm), _tile4(shm), 8)
    ckvn = (_rms(u[:, _C_MCKV:_C_MCKV + 128]) * kvag_ref[...]).astype(BF16)
    mkn = jnp.dot(ckvn, wukvk_ref[...], preferred_element_type=F32)
    mkn = _group_rms(mkn, gm, invn) * mknn_ref[...]
    mvv = jnp.dot(ckvn, wukvv_ref[...], preferred_element_type=F32)
    kr = _group_rms(u[:, _C_MKR:_C_MKR + 128], gm[:128, :128], invn[:, :128]) * mknr_ref[...]
    kr = _rope(kr, cm, slm, shm, 8)
    for hh in range(N_HEADS):
        mq_o[hh] = mq[:, 128 * hh:128 * hh + 128].astype(BF16)
        mk_o[hh] = (mkn[:, 128 * hh:128 * hh + 128] + kr).astype(BF16)
        mv_o[hh] = jnp.concatenate([mvv[:, 64 * hh:64 * hh + 64], ones64], axis=-1).astype(BF16)

    lx_o[...] = u[:, _C_LX:_C_LX + 256]
    g = u[:, _C_LG:_C_LG + 256]
    lg_o[...] = 0.5 * g * (1.0 + jnp.tanh(0.7978845608028654 * (g + 0.044715 * g * g * g)))


def _premix(x, mod, S, tm, consts, lw):
    T, D = x.shape
    nL = S // tm
    row = lambda w: pl.BlockSpec((tm, w), lambda i: (i, 0))
    hm = lambda hn, w: pl.BlockSpec((hn, tm, w), lambda i: (0, i, 0))
    in_specs = [
        row(D),
        pl.BlockSpec((1, 6, D), lambda i: (jnp.where(i >= nL, 1, 0), 0, 0)),
        _full((1, D)), _full((D, IN_COLS_R)), _full((256, 256)), _full((512, 512)),
        pl.BlockSpec((1, tm // GRID_W, 768), lambda i: (i, 0, 0)),
        pl.BlockSpec((1, tm, 768), lambda i: (jnp.where(i >= nL, 1, 0), 0, 0)),
        pl.BlockSpec((1, tm // GRID_W, 3 * MLA_PAD), lambda i: (i, 0, 0)),
        pl.BlockSpec((1, tm, 3 * MLA_PAD), lambda i: (jnp.where(i >= nL, 1, 0), 0, 0)),
        _full((1, 256)), _full((1, 256)), _full((1, 256)), _full((1, 128)),
        _full((1, 256)), _full((1, 128)),
        _full((1, 512)), _full((1, 512)), _full((1, 128)), _full((1, 512)),
        _full((256, 512)), _full((128, 512)), _full((128, 256)),
    ]
    out_specs = [row(256), row(256), row(256),
                 hm(4, 64), hm(2, 64), hm(2, 128),
                 hm(4, 128), hm(4, 128), hm(4, 128),
                 row(256), row(256)]
    sds = jax.ShapeDtypeStruct
    out_shape = [sds((T, 256), BF16), sds((T, 256), BF16), sds((T, 256), BF16),
                 sds((4, T, 64), BF16), sds((2, T, 64), BF16), sds((2, T, 128), BF16),
                 sds((4, T, 128), BF16), sds((4, T, 128), BF16), sds((4, T, 128), BF16),
                 sds((T, 256), F32), sds((T, 256), F32)]
    return pl.pallas_call(
        _premix_kernel, grid=(T // tm,), in_specs=in_specs, out_specs=out_specs,
        out_shape=out_shape, compiler_params=_params(1), name="premix",
    )(x, mod, lw["g1"], lw["w_in"], consts["bd64"], consts["gm"],
      consts["rope_q_row"], consts["rope_q_col"], consts["rope_m_row"], consts["rope_m_col"],
      lw["na_qn"], lw["na_kn"], lw["gqa_qn"], lw["gqa_kn"], lw["qa_g"], lw["kva_g"],
      lw["mla_qn"], lw["mla_knn"], lw["mla_knr"], consts["invn"],
      lw["wuq"], lw["wukv_k"], lw["wukv_v"])


def _flash_kernel(q_ref, k_ref, v_ref, o_ref, m_scr, acc_scr, *, group, rb):
    j = pl.program_id(1)
    n_kv = k_ref.shape[0]
    tq = q_ref.shape[1]

    @pl.when(j == 0)
    def _():
        m_scr[...] = jnp.full(m_scr.shape, NEG, F32)
        acc_scr[...] = jnp.zeros(acc_scr.shape, F32)

    chains = [(hq, slice(r0, r0 + rb)) for hq in range(n_kv * group) for r0 in range(0, tq, rb)]

    def scores(chain):
        hq, rows = chain
        return lax.dot_general(q_ref[hq, rows, :], k_ref[hq // group], (((1,), (1,)), ((), ())),
                               preferred_element_type=F32)

    s = scores(chains[0])
    for n, (hq, rows) in enumerate(chains):
        s_next = scores(chains[n + 1]) if n + 1 < len(chains) else None
        m_old = m_scr[hq, rows, :]
        m_new = jnp.maximum(m_old, jnp.max(s, axis=-1, keepdims=True))
        alpha = jnp.exp2(m_old - m_new)
        p = jnp.exp2(s - m_new).astype(BF16)
        acc_scr[hq, rows, :] = (alpha * acc_scr[hq, rows, :]
                                + jnp.dot(p, v_ref[hq // group], preferred_element_type=F32))
        m_scr[hq, rows, :] = m_new
        s = s_next

    @pl.when(j == pl.num_programs(1) - 1)
    def _():
        outs = []
        for hq in range(n_kv * group):
            a = acc_scr[hq]
            outs.append(a[:, :HEAD_DIM] / a[:, HEAD_DIM:HEAD_DIM + 1])
        o_ref[...] = jnp.concatenate(outs, axis=-1)


def _flash(q, k, v, tq, tk, n_q, q_blk0, n_kv, kv_blk0):
    hq, _, d = q.shape
    hk = k.shape[0]
    group = hq // hk
    return pl.pallas_call(
        functools.partial(_flash_kernel, group=group, rb=min(FLASH_ROW_BLOCK, tq)),
        grid=(n_q, n_kv),
        in_specs=[pl.BlockSpec((hq, tq, d), lambda i, j: (0, q_blk0 + i, 0)),
                  pl.BlockSpec((hk, tk, d), lambda i, j: (0, kv_blk0 + j, 0)),
                  pl.BlockSpec((hk, tk, 128), lambda i, j: (0, kv_blk0 + j, 0))],
        out_specs=pl.BlockSpec((tq, hq * HEAD_DIM), lambda i, j: (i, 0)),
        out_shape=jax.ShapeDtypeStruct((n_q * tq, hq * HEAD_DIM), F32),
        scratch_shapes=[pltpu.VMEM((hq, tq, 1), F32), pltpu.VMEM((hq, tq, 128), F32)],
        compiler_params=_params(2), name="flash_attn",
    )(q, k, v)


def _na_kernel(q_ref, k_ref, v_ref, tab_ref, o_ref, *, rows, n_ctx):
    i = pl.program_id(0)
    s_lat = rows * GRID_W
    kc = k_ref[pl.ds(s_lat, n_ctx), :]
    vc = v_ref[pl.ds(s_lat, n_ctx), :]
    nq = N_HEADS * GRID_W
    same_head = (lax.broadcasted_iota(jnp.int32, (nq, GROUP_W), 0) // GRID_W
                 == lax.broadcasted_iota(jnp.int32, (nq, GROUP_W), 1) // HEAD_DIM)
    q = q_ref[...].astype(F32)
    q_bd = jnp.where(same_head, jnp.concatenate([q] * N_HEADS, axis=0), 0.0).astype(BF16)
    nt = (((1,), (1,)), ((), ()))
    s_cx = lax.dot_general(q_bd, kc, nt, preferred_element_type=F32)

    def heads_out(res):
        res = jnp.where(same_head, res, 0.0)
        out = res[0:GRID_W]
        for h in range(1, N_HEADS):
            out = out + res[GRID_W * h:GRID_W * (h + 1)]
        return out

    @pl.when(i < rows)
    def _():
        r0 = jnp.clip(i - NA_KH // 2, 0, rows - NA_KH)
        start = pl.multiple_of(r0 * GRID_W, GRID_W)
        ks = k_ref[pl.ds(start, NA_KH * GRID_W), :]
        vs = v_ref[pl.ds(start, NA_KH * GRID_W), :]
        s_nb = (lax.dot_general(q_bd, ks, nt, preferred_element_type=F32)
                + tab_ref[i - r0].reshape(nq, NA_KH * GRID_W))
        m = jnp.maximum(jnp.max(s_nb, axis=-1, keepdims=True), jnp.max(s_cx, axis=-1, keepdims=True))
        p_nb = jnp.exp(s_nb - m)
        p_cx = jnp.exp(s_cx - m)
        l = jnp.sum(p_nb, axis=-1, keepdims=True) + jnp.sum(p_cx, axis=-1, keepdims=True)
        res = (jnp.dot(p_nb.astype(BF16), vs, preferred_element_type=F32)
               + jnp.dot(p_cx.astype(BF16), vc, preferred_element_type=F32))
        o_ref[...] = heads_out(res / l)

    @pl.when(i >= rows)
    def _():
        m = jnp.max(s_cx, axis=-1, keepdims=True)
        p_cx = jnp.exp(s_cx - m)
        l = jnp.sum(p_cx, axis=-1, keepdims=True)
        res = jnp.dot(p_cx.astype(BF16), vc, preferred_element_type=F32)
        o_ref[...] = heads_out(res / l)


def _neighbourhood(q, k, v, tab, S):
    T = q.shape[0]
    rows = S // GRID_W
    n_ctx = T - S
    resident = lambda shape: pl.BlockSpec(shape, lambda i: (0,) * len(shape),
                                          pipeline_mode=pl.Buffered(1))
    return pl.pallas_call(
        functools.partial(_na_kernel, rows=rows, n_ctx=n_ctx),
        grid=(T // GRID_W,),
        in_specs=[pl.BlockSpec((GRID_W, 256), lambda i: (i, 0)),
                  resident((T, 256)), resident((T, 256)),
                  resident((NA_KH, N_HEADS, GRID_W, NA_KH * GRID_W))],
        out_specs=pl.BlockSpec((GRID_W, 256), lambda i: (i, 0)),
        out_shape=jax.ShapeDtypeStruct((T, 256), F32),
        compiler_params=_params(1), name="nbr_attn",
    )(q, k, v, tab)


def _lru_kernel(x_ref, xp_ref, xn_ref, cw_ref, cb_ref, wa_ref, ba_ref, wi_ref, bi_ref,
                lam_ref, y_ref, h_scr, *, reverse, n_lat, n_ctx):
    j = pl.program_id(0)
    tc = x_ref.shape[0]
    n_all = n_lat + n_ctx
    if reverse:
        chunk = jnp.where(j < n_ctx, n_all - 1 - j, n_all - 1 - j)
    else:
        chunk = jnp.where(j < n_ctx, n_lat + j, j - n_ctx)
    has_prev = jnp.logical_and(chunk != 0, chunk != n_lat)
    has_next = jnp.logical_and(chunk != n_lat - 1, chunk != n_all - 1)

    @pl.when(j == 0)
    def _():
        h_scr[...] = jnp.zeros(h_scr.shape, F32)

    xp = jnp.where(has_prev, xp_ref[...], 0.0)
    xn = jnp.where(has_next, xn_ref[...], 0.0)
    xe = jnp.concatenate([xp, x_ref[...], xn], axis=0)
    ne = tc + 16
    cw = cw_ref[...]
    xc = (cw[0:1] * pltpu.roll(xe, 1, 0)[8:8 + tc]
          + cw[1:2] * xe[8:8 + tc]
          + cw[2:3] * pltpu.roll(xe, ne - 1, 0)[8:8 + tc]
          + cw[3:4] * pltpu.roll(xe, ne - 2, 0)[8:8 + tc]
          + cb_ref[...])

    r = jax.nn.sigmoid(jnp.dot(xc, wa_ref[0], preferred_element_type=F32, precision=HIGHEST)
                       + ba_ref[0])
    gi = jax.nn.sigmoid(jnp.dot(xc, wi_ref[0], preferred_element_type=F32, precision=HIGHEST)
                        + bi_ref[0])
    z = -lam_ref[0]
    softplus = jnp.maximum(z, 0.0) + jnp.log1p(jnp.exp(-jnp.abs(z)))
    log_a = -LRU_C * r * softplus
    a = jnp.exp(log_a)
    b = jnp.sqrt(-jnp.tanh(log_a) * (a * a + 1.0)) * (gi * xc)

    rowi = lax.broadcasted_iota(jnp.int32, (tc, 1), 0)
    s = 1
    while s < tc:
        if reverse:
            keep = rowi < tc - s
            a_s = jnp.where(keep, pltpu.roll(a, tc - s, 0), 1.0)
            b_s = jnp.where(keep, pltpu.roll(b, tc - s, 0), 0.0)
        else:
            keep = rowi >= s
            a_s = jnp.where(keep, pltpu.roll(a, s, 0), 1.0)
            b_s = jnp.where(keep, pltpu.roll(b, s, 0), 0.0)
        b = a * b_s + b
        a = a * a_s
        s *= 2
    hcur = b + a * h_scr[...]
    y_ref[...] = hcur
    h_scr[...] = hcur[0:1] if reverse else hcur[tc - 1:tc]


def _lru_scan(lx, lw, S, tc, reverse):
    T = lx.shape[0]
    n_lat, n_ctx = S // tc, (T - S) // tc
    n_all = n_lat + n_ctx
    d = 1 if reverse else 0
    t8 = tc // 8

    def chunk_of(j):
        if reverse:
            return n_all - 1 - j
        return jnp.where(j < n_ctx, n_lat + j, j - n_ctx)

    dspec = lambda shape: pl.BlockSpec((1,) + shape, lambda j: (d,) + (0,) * len(shape))
    return pl.pallas_call(
        functools.partial(_lru_kernel, reverse=reverse, n_lat=n_lat, n_ctx=n_ctx),
        grid=(n_all,),
        in_specs=[pl.BlockSpec((tc, 256), lambda j: (chunk_of(j), 0)),
                  pl.BlockSpec((8, 256), lambda j: (jnp.maximum(chunk_of(j) * t8 - 1, 0), 0)),
                  pl.BlockSpec((8, 256),
                               lambda j: (jnp.minimum((chunk_of(j) + 1) * t8, T // 8 - 1), 0)),
                  _full((4, 256)), _full((1, 256)),
                  dspec((256, 256)), dspec((1, 256)), dspec((256, 256)), dspec((1, 256)),
                  dspec((1, 256))],
        out_specs=pl.BlockSpec((tc, 256), lambda j: (chunk_of(j), 0)),
        out_shape=jax.ShapeDtypeStruct((T, 256), F32),
        scratch_shapes=[pltpu.VMEM((1, 256), F32)],
        compiler_params=_params(1), name="rglru_bwd" if reverse else "rglru_fwd",
    )(lx, lx, lx, lw["conv_w"], lw["conv_b"], lw["wa"], lw["ba"], lw["wi"], lw["bi"], lw["lam"])


def _merge_kernel(x_ref, mod_ref, oa_ref, obl_ref, obc_ref, ocl_ref, occ_ref, yf_ref, yb_ref,
                  lg_ref, gg_ref, wout_ref, g2_ref, rw_ref, rb_ref, lt_ref, ut_ref,
                  x1_o, tok_o, slot_o, gate_o, cnt_o, *, n_lat_tiles):
    i = pl.program_id(0)
    tm = x_ref.shape[0]
    is_ctx = i >= n_lat_tiles

    gg = gg_ref[...]
    od = (yf_ref[...] + yb_ref[...]) * lg_ref[...]
    ob = jnp.where(is_ctx, obc_ref[...], obl_ref[...])
    oc = jnp.where(is_ctx, occ_ref[...], ocl_ref[...])
    parts = [oa_ref[...], ob, oc, od]
    ycat = jnp.concatenate(
        [(_rms(p) * gg[:, GROUP_W * n:GROUP_W * (n + 1)]).astype(BF16) for n, p in enumerate(parts)],
        axis=-1)
    y = jnp.dot(ycat, wout_ref[...], preferred_element_type=F32)
    gt1 = mod_ref[0, 2:3, :]
    sh2 = mod_ref[0, 3:4, :]
    sc2 = mod_ref[0, 4:5, :]
    x1 = x_ref[...] + gt1 * y
    x1_o[...] = x1
    tok = _rms(x1) * g2_ref[...] * (1.0 + sc2) + sh2
    tok_o[...] = tok.astype(BF16)

    logits = jnp.dot(tok, rw_ref[...], preferred_element_type=F32, precision=HIGHEST) + rb_ref[...]
    lane = lax.broadcasted_iota(jnp.int32, (tm, LANES), 1)
    v_acc = jnp.zeros((tm, LANES), F32)
    onehots = []
    v0 = None
    work = logits
    for k in range(TOP_K):
        vk = jnp.max(work, axis=-1, keepdims=True)
        ek = jnp.min(jnp.where(work == vk, lane, LANES), axis=-1, keepdims=True)
        sel = lane == ek
        onehots.append(sel)
        work = jnp.where(sel, NEG * 2.0, work)
        if k == 0:
            v0 = vk
        v_acc = jnp.where(lane == k, jnp.exp(vk - v0), v_acc)
    gate_o[...] = v_acc / jnp.sum(v_acc, axis=-1, keepdims=True)

    oh = [jnp.where(o, 1.0, 0.0) for o in onehots]
    oh_all = oh[0] + oh[1] + oh[2] + oh[3]
    cnt = jnp.sum(oh_all, axis=0, keepdims=True)
    run = jnp.floor((cnt + (RUN_ALIGN - 1.0)) * (1.0 / RUN_ALIGN)) * RUN_ALIGN
    run_start = jnp.dot(jnp.broadcast_to(run, (8, LANES)).astype(BF16), ut_ref[...],
                        preferred_element_type=F32)[0:1]
    excl = jnp.dot(lt_ref[...], oh_all.astype(BF16), preferred_element_type=F32) + run_start
    slot = jnp.full((tm, LANES), -1.0, F32)
    for k in range(TOP_K):
        sk = jnp.sum(oh[k] * excl, axis=-1, keepdims=True)
        slot = jnp.where(lane == k, sk, slot)
    slot_o[...] = slot.astype(jnp.int32)
    cnt_o[0] = jnp.broadcast_to(cnt, (8, LANES))


def _merge(x, mod, oa, ob_lat, ob_ctx, oc_lat, oc_ctx, yf, yb, lg, S, tm, consts, lw):
    T, D = x.shape
    nL = S // tm
    nt = T // tm
    row = lambda w: pl.BlockSpec((tm, w), lambda i: (i, 0))
    lat = pl.BlockSpec((tm, 256), lambda i: (jnp.minimum(i, nL - 1), 0))
    ctx = pl.BlockSpec((tm, 256), lambda i: (jnp.maximum(i - nL, 0), 0))
    sds = jax.ShapeDtypeStruct
    return pl.pallas_call(
        functools.partial(_merge_kernel, n_lat_tiles=nL), grid=(nt,),
        in_specs=[row(D), pl.BlockSpec((1, 6, D), lambda i: (jnp.where(i >= nL, 1, 0), 0, 0)),
                  row(256), lat, ctx, lat, ctx, row(256), row(256), row(256),
                  _full((1, D)), _full((D, D)), _full((1, D)), _full((D, LANES)),
                  _full((1, LANES)), _full((tm, tm)), _full((LANES, LANES))],
        out_specs=[row(D), row(D), row(LANES), row(LANES),
                   pl.BlockSpec((1, 8, LANES), lambda i: (i, 0, 0))],
        out_shape=[sds((T, D), F32), sds((T, D), BF16), sds((T, LANES), jnp.int32),
                   sds((T, LANES), F32), sds((nt, 8, LANES), F32)],
        compiler_params=_params(1), name="merge_router",
    )(x, mod, oa, ob_lat, ob_ctx, oc_lat, oc_ctx, yf, yb, lg, lw["grp_g"], lw["w_out"], lw["g2"],
      lw["router_w"], lw["router_b"], consts["lt"], consts["ut"])


def _routing_meta(cnt, bm, n_rows):
    c = cnt[:, 0, :N_EXPERTS].astype(jnp.int32)
    nt = c.shape[0]
    run = (c + RUN_ALIGN - 1) // RUN_ALIGN * RUN_ALIGN
    tile_off = jnp.cumsum(run, axis=0) - run
    tot = jnp.sum(run, axis=0)
    padded = (tot + bm - 1) // bm * bm
    seg_end = jnp.cumsum(padded)
    seg_start = seg_end - padded
    g_start = seg_start[None, :] + tile_off
    l_start = jnp.cumsum(run, axis=1) - run
    per_size, done = [], jnp.zeros_like(run)
    for size in CHUNK_ROWS:
        n = (run - done) // size
        per_size.append(jnp.sum(n, axis=1, keepdims=True))
        done = done + n * size
    n_tot = jnp.sum(run, axis=1, keepdims=True) // RUN_ALIGN
    meta = jnp.concatenate(
        [g_start, l_start, run] + per_size
        + [n_tot, jnp.zeros((nt, LANES - 3 * N_EXPERTS - len(CHUNK_ROWS) - 1), jnp.int32)], axis=1)
    n_zero = (padded - tot) // RUN_ALIGN
    zmeta = jnp.concatenate([seg_start + tot, n_zero, jnp.sum(n_zero)[None], seg_end[-1:] // bm,
                             jnp.zeros((LANES - 2 * N_EXPERTS - 2,), jnp.int32)])
    blk_row = jnp.arange(n_rows // bm, dtype=jnp.int32) * bm
    blk_expert = jnp.minimum(jnp.sum((seg_end[None, :] <= blk_row[:, None]).astype(jnp.int32), axis=1),
                             N_EXPERTS - 1)
    n_used = seg_end[-1:] // bm
    return (meta.reshape(nt, 1, LANES).astype(jnp.int32), zmeta.reshape(1, 1, LANES).astype(jnp.int32),
            blk_expert.astype(jnp.int32), n_used.astype(jnp.int32))


def _run_copies(meta_ref, make_copy):
    def per_expert(e, c):
        g = meta_ref[0, 0, e]
        l = meta_ref[0, 0, N_EXPERTS + e]
        rows = meta_ref[0, 0, 2 * N_EXPERTS + e]
        big = CHUNK_ROWS[0]

        def issue(cc, c2):
            off = big * cc
            make_copy(pl.multiple_of(g + off, RUN_ALIGN), pl.multiple_of(l + off, RUN_ALIGN),
                      big).start()
            return c2

        c = lax.fori_loop(0, rows // big, issue, c)
        done = rows // big * big
        for size in CHUNK_ROWS[1:]:
            use = (rows - done) >= size

            @pl.when(use)
            def _(size=size, done=done):
                make_copy(pl.multiple_of(g + done, RUN_ALIGN), pl.multiple_of(l + done, RUN_ALIGN),
                          size).start()

            done = done + jnp.where(use, size, 0)
        return c

    lax.fori_loop(0, N_EXPERTS, per_expert, 0)


def _drain(meta_ref, make_copy):
    for n, size in enumerate(CHUNK_ROWS):
        def body(_, c, size=size):
            make_copy(0, 0, size).wait()
            return c

        lax.fori_loop(0, meta_ref[0, 0, 3 * N_EXPERTS + n], body, 0)


def _dispatch_kernel(meta_ref, prev_meta_ref, zmeta_ref, slot_ref, tok_ref, xin_ref,
                     xs_scr, z_scr, sem):
    i = pl.program_id(0)
    tm = tok_ref.shape[0]
    n_slot = xs_scr.shape[1]
    buf = lax.rem(i, 2)
    slot_t = jnp.transpose(slot_ref[...].astype(F32))
    srow = lax.broadcasted_iota(jnp.int32, (n_slot, tm), 0).astype(F32)
    perm = jnp.zeros((n_slot, tm), F32)
    for k in range(TOP_K):
        perm = perm + jnp.where(srow == slot_t[k:k + 1, :], 1.0, 0.0)
    xs_scr[buf] = jnp.dot(perm.astype(BF16), tok_ref[...], preferred_element_type=F32)

    def to_global(g_row, l_row, size):
        return pltpu.make_async_copy(xs_scr.at[buf, pl.ds(l_row, size)],
                                     xin_ref.at[pl.ds(g_row, size)], sem)

    @pl.when(i > 0)
    def _():
        _drain(prev_meta_ref, to_global)

    _run_copies(meta_ref, to_global)

    @pl.when(i == pl.num_programs(0) - 1)
    def _():
        _drain(meta_ref, to_global)
        z_scr[...] = jnp.zeros(z_scr.shape, F32)

        def zero_copy(g_row):
            return pltpu.make_async_copy(z_scr.at[pl.ds(0, RUN_ALIGN)],
                                         xin_ref.at[pl.ds(g_row, RUN_ALIGN)], sem)

        def per_expert(e, c):
            g = zmeta_ref[0, 0, e]

            def issue(cc, c2):
                zero_copy(pl.multiple_of(g + RUN_ALIGN * cc, RUN_ALIGN)).start()
                return c2

            return lax.fori_loop(0, zmeta_ref[0, 0, N_EXPERTS + e], issue, c)

        lax.fori_loop(0, N_EXPERTS, per_expert, 0)

        def drain(_, c):
            zero_copy(0).wait()
            return c

        lax.fori_loop(0, zmeta_ref[0, 0, 2 * N_EXPERTS], drain, 0)

        bm = z_scr.shape[0]

        def block_copy(b):
            return pltpu.make_async_copy(z_scr, xin_ref.at[pl.ds(pl.multiple_of(b * bm, bm), bm)], sem)

        def issue_block(b, c):
            block_copy(b).start()
            return c

        def drain_block(b, c):
            block_copy(b).wait()
            return c

        n_used = zmeta_ref[0, 0, 2 * N_EXPERTS + 1]
        lax.fori_loop(n_used, xin_ref.shape[0] // bm, issue_block, 0)
        lax.fori_loop(n_used, xin_ref.shape[0] // bm, drain_block, 0)


def _dispatch(tok, slot, meta, zmeta, n_rows, tm, bm):
    T, D = tok.shape
    n_slot = TOP_K * tm + N_EXPERTS * RUN_ALIGN
    smem = lambda idx: pl.BlockSpec((1, 1, LANES), idx, memory_space=pltpu.SMEM)
    return pl.pallas_call(
        _dispatch_kernel,
        grid=(T // tm,),
        in_specs=[smem(lambda i: (i, 0, 0)), smem(lambda i: (jnp.maximum(i - 1, 0), 0, 0)),
                  smem(lambda i: (0, 0, 0)),
                  pl.BlockSpec((tm, LANES), lambda i: (i, 0)),
                  pl.BlockSpec((tm, D), lambda i: (i, 0))],
        out_specs=pl.BlockSpec(memory_space=pl.ANY),
        out_shape=jax.ShapeDtypeStruct((n_rows, D), F32),
        scratch_shapes=[pltpu.VMEM((2, n_slot, D), F32), pltpu.VMEM((bm, D), F32),
                        pltpu.SemaphoreType.DMA(())],
        compiler_params=_params(1), name="moe_dispatch",
    )(meta, meta, zmeta, slot, tok)


def _expert_kernel(be_ref, nu_ref, x_ref, wgu_ref, bgu_ref, wd_ref, bd_ref, y_ref,
                   wgu_bf, wd_bf):
    i = pl.program_id(0)
    d_e = wd_ref.shape[1]

    @pl.when(i < nu_ref[0])
    def _():
        @pl.when(jnp.logical_or(i == 0, be_ref[i] != be_ref[jnp.maximum(i - 1, 0)]))
        def _():
            wgu_bf[...] = wgu_ref[0].astype(BF16)
            wd_bf[...] = wd_ref[0].astype(BF16)

        gu = jnp.dot(x_ref[...].astype(BF16), wgu_bf[...], preferred_element_type=F32) + bgu_ref[0]
        x_glu = jnp.minimum(gu[:, :d_e], SWIGLU_LIMIT)
        x_lin = jnp.clip(gu[:, d_e:], -SWIGLU_LIMIT, SWIGLU_LIMIT)
        act = x_glu * jax.nn.sigmoid(SWIGLU_ALPHA * x_glu) * (x_lin + 1.0)
        y_ref[...] = jnp.dot(act.astype(BF16), wd_bf[...], preferred_element_type=F32) + bd_ref[0]

    @pl.when(i >= nu_ref[0])
    def _():
        y_ref[...] = jnp.zeros(y_ref.shape, F32)


def _experts(xin, blk_expert, n_used, layer, w_gu, b_gu, w_down, b_down, bm):
    P, D = xin.shape
    d2 = w_gu.shape[-1]
    d_e = d2 // 2
    blk = lambda i, be, nu: jnp.minimum(i, nu[0] - 1)
    wspec = lambda r, c: pl.BlockSpec((None, 1, r, c),
                                      lambda i, be, nu: (layer, be[blk(i, be, nu)], 0, 0))
    grid_spec = pltpu.PrefetchScalarGridSpec(
        num_scalar_prefetch=2, grid=(P // bm,),
        in_specs=[pl.BlockSpec((bm, D), lambda i, be, nu: (blk(i, be, nu), 0)),
                  wspec(D, d2), wspec(1, d2), wspec(d_e, D), wspec(1, D)],
        out_specs=pl.BlockSpec((bm, D), lambda i, be, nu: (i, 0)),
        scratch_shapes=[pltpu.VMEM((D, d2), BF16), pltpu.VMEM((d_e, D), BF16)])
    return pl.pallas_call(
        _expert_kernel, grid_spec=grid_spec,
        out_shape=jax.ShapeDtypeStruct((P, D), F32),
        compiler_params=_params(1), name="moe_experts",
    )(blk_expert, n_used, xin, w_gu, b_gu, w_down, b_down)


def _combine_kernel(meta_ref, next_meta_ref, slot_ref, gate_ref, y_ref, x_ref, mod_ref, o_ref,
                    ybuf, sem):
    i = pl.program_id(0)
    tm = x_ref.shape[0]
    _, n_slot, d = ybuf.shape
    buf = lax.rem(i, 2)

    def fetch(m_ref, b):
        def to_local(g_row, l_row, size):
            return pltpu.make_async_copy(y_ref.at[pl.ds(g_row, size)],
                                         ybuf.at[b, pl.ds(l_row, size)], sem)

        _run_copies(m_ref, to_local)
        n_tot = m_ref[0, 0, 3 * N_EXPERTS + len(CHUNK_ROWS)]

        def zero_tail(cc, c):
            ybuf[b, pl.ds(pl.multiple_of(RUN_ALIGN * cc, RUN_ALIGN), RUN_ALIGN), :] = jnp.zeros(
                (RUN_ALIGN, d), F32)
            return c

        lax.fori_loop(n_tot, n_slot // RUN_ALIGN, zero_tail, 0)
        return to_local

    @pl.when(i == 0)
    def _():
        fetch(meta_ref, buf)

    def wait_shape(g_row, l_row, size):
        return pltpu.make_async_copy(y_ref.at[pl.ds(g_row, size)],
                                     ybuf.at[buf, pl.ds(l_row, size)], sem)

    _drain(meta_ref, wait_shape)

    @pl.when(i + 1 < pl.num_programs(0))
    def _():
        fetch(next_meta_ref, 1 - buf)

    slot = slot_ref[...]
    gate = gate_ref[...]
    col = lax.broadcasted_iota(jnp.int32, (tm, n_slot), 1)
    w = jnp.zeros((tm, n_slot), F32)
    for k in range(TOP_K):
        w = w + jnp.where(col == slot[:, k:k + 1], gate[:, k:k + 1], 0.0)
    w_hi = w.astype(BF16)
    w_lo = (w - w_hi.astype(F32)).astype(BF16)
    yb = ybuf[buf].astype(BF16)
    f = (jnp.dot(w_hi, yb, preferred_element_type=F32)
         + jnp.dot(w_lo, yb, preferred_element_type=F32))
    o_ref[...] = x_ref[...] + mod_ref[0, 5:6, :] * f


def _combine(y, slot, gate, meta, x1, mod, S, tm):
    T, D = x1.shape
    nL = S // tm
    n_slot = TOP_K * tm + N_EXPERTS * RUN_ALIGN
    return pl.pallas_call(
        _combine_kernel,
        grid=(T // tm,),
        in_specs=[pl.BlockSpec((1, 1, LANES), lambda i: (i, 0, 0), memory_space=pltpu.SMEM),
                  pl.BlockSpec((1, 1, LANES), lambda i: (jnp.minimum(i + 1, T // tm - 1), 0, 0),
                               memory_space=pltpu.SMEM),
                  pl.BlockSpec((tm, LANES), lambda i: (i, 0)),
                  pl.BlockSpec((tm, LANES), lambda i: (i, 0)),
                  pl.BlockSpec(memory_space=pl.ANY),
                  pl.BlockSpec((tm, D), lambda i: (i, 0)),
                  pl.BlockSpec((1, 6, D), lambda i: (jnp.where(i >= nL, 1, 0), 0, 0))],
        out_specs=pl.BlockSpec((tm, D), lambda i: (i, 0)),
        out_shape=jax.ShapeDtypeStruct((T, D), F32),
        scratch_shapes=[pltpu.VMEM((2, n_slot, D), F32), pltpu.SemaphoreType.DMA(())],
        compiler_params=_params(1), name="moe_combine",
    )(meta, meta, slot, gate, y, x1, mod)


def _rope_tables(S, T, tm, rot_dim, width, lane0, reps):
    h = rot_dim // 4
    ax = rot_dim // 2
    inv = ROPE_THETA ** (-jnp.arange(0, ax, 2, dtype=F32) / ax)

    def parts(n):
        ang = jnp.arange(n, dtype=F32)[:, None] * inv
        return jnp.cos(ang), jnp.sin(ang), jnp.zeros((n, h), F32)

    def place(blocks, fill):
        n = blocks[0].shape[0]
        tab = jnp.concatenate(
            [jnp.full((n, lane0), fill, F32)] + list(blocks)
            + [jnp.full((n, width - lane0 - rot_dim), fill, F32)], axis=-1)
        return jnp.tile(tab, (1, reps))

    n_rows = S // GRID_W
    c, s, z = parts(n_rows)
    row = jnp.concatenate([place([c, c, z, z], 0.0), place([z, s, z, z], 0.0),
                           place([-s, z, z, z], 0.0)], axis=-1)
    row = jnp.concatenate([row, jnp.zeros(((T - S) // GRID_W, row.shape[1]), F32)], axis=0)
    c, s, z = parts(GRID_W)
    col = jnp.concatenate([place([z, z, c, c], 1.0), place([z, z, z, s], 0.0),
                           place([z, z, -s, z], 0.0)], axis=-1)
    col = jnp.tile(col, (tm // GRID_W, 1))
    ident = jnp.concatenate([jnp.ones((tm, reps * width), F32),
                             jnp.zeros((tm, 2 * reps * width), F32)], axis=-1)
    return row.reshape(T // tm, tm // GRID_W, -1), jnp.stack([col, ident])


def _constants(S, T, tm):
    rope_q_row, rope_q_col = _rope_tables(S, T, tm, HEAD_DIM, HEAD_DIM, 0, N_HEADS)
    rope_m_row, rope_m_col = _rope_tables(S, T, tm, MLA_ROPE, MLA_PAD, MLA_NOPE, 1)
    bd64 = np.kron(np.eye(4, dtype=np.float32), np.ones((64, 64), np.float32))
    sizes = [MLA_NOPE, MLA_ROPE, MLA_PAD - MLA_NOPE - MLA_ROPE] * N_HEADS
    gid = np.repeat(np.arange(len(sizes)), sizes)
    gm = (gid[:, None] == gid[None, :]).astype(np.float32)
    invn = (1.0 / np.repeat(np.asarray(sizes, np.float32), sizes))[None, :]
    lt = np.tril(np.ones((tm, tm), np.float32), -1)
    ut = np.triu(np.ones((LANES, LANES), np.float32), 1)
    return dict(rope_q_row=rope_q_row, rope_q_col=rope_q_col,
                rope_m_row=rope_m_row, rope_m_col=rope_m_col,
                bd64=jnp.asarray(bd64, BF16), gm=jnp.asarray(gm, BF16),
                invn=jnp.asarray(invn), lt=jnp.asarray(lt, BF16), ut=jnp.asarray(ut, BF16))


def _na_bias_tables(rpb):
    qc = np.arange(GRID_W)[:, None]
    kc = np.arange(GRID_W)[None, :]
    c0 = np.clip(qc - NA_KW // 2, 0, GRID_W - NA_KW)
    valid = (kc >= c0) & (kc < c0 + NA_KW)
    sel = ((kc - qc + NA_KW - 1)[:, :, None] == np.arange(2 * NA_KW - 1)) & valid[:, :, None]
    toep = jnp.einsum("lhab,qkb->lhaqk", rpb, jnp.asarray(sel, F32), precision=HIGHEST)
    toep = jnp.where(jnp.asarray(valid)[None, None, None], toep, NEG)
    L, H = rpb.shape[:2]
    tabs = []
    for off in range(NA_KH):
        rows = toep[:, :, NA_KH - 1 - off:2 * NA_KH - 1 - off]
        tabs.append(rows.transpose(0, 1, 3, 2, 4).reshape(L, H, GRID_W, NA_KH * GRID_W))
    return jnp.stack(tabs, axis=1)


def _block_diag(w):
    n, bw, _ = w.shape
    eye = jnp.eye(n, dtype=w.dtype)
    return (eye[:, None, :, None] * w[:, :, None, :]).reshape(n * bw, n * bw)


def _layer_weights(l, p):
    D = p["w_in"].shape[1]
    w_in = p["w_in"][l]
    offs = np.cumsum([0, 256, 256, 256, 256, 128, 128, 256, 128, 32, 256, 256])
    seg = lambda n: w_in[:, offs[n]:offs[n + 1]]
    z = lambda n: jnp.zeros((D, n), F32)
    w_in_r = jnp.concatenate(
        [seg(0), seg(1), seg(2), seg(3), seg(4), seg(5), seg(6), seg(7), seg(9), seg(10),
         z(MLA_NOPE), seg(8), z(MLA_PAD - MLA_NOPE - MLA_ROPE)], axis=-1).astype(BF16)
    sc = HEAD_DIM ** -0.5
    sc_m = (MLA_NOPE + MLA_ROPE) ** -0.5
    t4 = lambda g: jnp.tile(g, N_HEADS)[None, :]
    wuq = p["mla_wuq"][l].reshape(-1, N_HEADS, MLA_NOPE + MLA_ROPE)
    wuq = jnp.pad(wuq, ((0, 0), (0, 0), (0, MLA_PAD - MLA_NOPE - MLA_ROPE)))
    wukv = p["mla_wukv"][l].reshape(-1, N_HEADS, MLA_NOPE + HEAD_DIM)
    wukv_k = jnp.pad(wukv[:, :, :MLA_NOPE], ((0, 0), (0, 0), (0, MLA_PAD - MLA_NOPE)))
    qn, kn = p["mla_qn"][l], p["mla_kn"][l]
    padq = jnp.pad(qn * (sc_m * LOG2E), (0, MLA_PAD - MLA_NOPE - MLA_ROPE))
    padkn = jnp.pad(kn[:MLA_NOPE], (0, MLA_PAD - MLA_NOPE))
    padkr = jnp.pad(kn[MLA_NOPE:], (MLA_NOPE, MLA_PAD - MLA_NOPE - MLA_ROPE))
    rw =jnp.pad(p["router_w"][l], ((0, 0), (0, LANES - N_EXPERTS)))
    rb = jnp.pad(p["router_b"][l], (0, LANES - N_EXPERTS), constant_values=NEG)[None, :]
    return dict(
        g1=p["norm1_g"][l][None, :], g2=p["norm2_g"][l][None, :], w_in=w_in_r,
        na_qn=t4(p["na_qn"][l] * sc), na_kn=t4(p["na_kn"][l]),
        gqa_qn=t4(p["gqa_qn"][l] * (sc * LOG2E)), gqa_kn=jnp.tile(p["gqa_kn"][l], GQA_KV_HEADS)[None, :],
        qa_g=p["mla_qa_g"][l][None, :], kva_g=p["mla_kva_g"][l][None, :],
        mla_qn=t4(padq), mla_knn=t4(padkn), mla_knr=padkr[None, :],
        wuq=wuq.reshape(-1, N_HEADS * MLA_PAD).astype(BF16),
        wukv_k=wukv_k.reshape(-1, N_HEADS * MLA_PAD).astype(BF16),
        wukv_v=wukv[:, :, MLA_NOPE:].reshape(-1, N_HEADS * HEAD_DIM).astype(BF16),
        conv_w=p["lru_conv_w"][l][:, 0, :], conv_b=p["lru_conv_b"][l][None, :],
        wa=jnp.stack([_block_diag(p["lru_wa"][l][d]) for d in range(2)]),
        wi=jnp.stack([_block_diag(p["lru_wi"][l][d]) for d in range(2)]),
        ba=p["lru_ba"][l][:, None, :], bi=p["lru_bi"][l][:, None, :],
        lam=p["lru_lam"][l][:, None, :],
        grp_g=p["grp_g"][l][None, :], w_out=p["w_out"][l].astype(BF16),
        router_w=rw, router_b=rb,
    )


def _tiles(S, C):
    tm = min(256, C)
    tq = min(1024, S)
    T = S + C
    tk = next(t for t in (3328, 1280, 640, 256, 128) if T % t == 0)
    return dict(tm=tm, tq=tq, tk=tk, tc=min(256, C), bm=512)


def kernel(x, c, ctx, c_ctx, ada_w, ada_b, norm1_g, norm2_g, w_in, na_qn, na_kn, na_rpb, gqa_qn, gqa_kn, mla_qa_g, mla_kva_g, mla_wuq, mla_wukv, mla_qn, mla_kn, lru_conv_w, lru_conv_b, lru_wa, lru_ba, lru_wi, lru_bi, lru_lam, grp_g, w_out, router_w, router_b, exp_w_gu, exp_b_gu, exp_w_down, exp_b_down):
    p = dict(norm1_g=norm1_g, norm2_g=norm2_g, w_in=w_in, na_qn=na_qn, na_kn=na_kn, na_rpb=na_rpb,
             gqa_qn=gqa_qn, gqa_kn=gqa_kn, mla_qa_g=mla_qa_g, mla_kva_g=mla_kva_g,
             mla_wuq=mla_wuq, mla_wukv=mla_wukv, mla_qn=mla_qn, mla_kn=mla_kn,
             lru_conv_w=lru_conv_w, lru_conv_b=lru_conv_b, lru_wa=lru_wa, lru_ba=lru_ba,
             lru_wi=lru_wi, lru_bi=lru_bi, lru_lam=lru_lam, grp_g=grp_g, w_out=w_out,
             router_w=router_w, router_b=router_b, exp_w_gu=exp_w_gu, exp_b_gu=exp_b_gu,
             exp_w_down=exp_w_down, exp_b_down=exp_b_down)
    B, S, D = x.shape
    assert B == 1 and S % GRID_W == 0
    C = ctx.shape[1]
    T = S + C
    L = ada_w.shape[0]
    ts = _tiles(S, C)
    tm, bm = ts["tm"], ts["bm"]
    consts = _constants(S, T, tm)

    c8 = jnp.zeros((8, D), F32).at[0].set(c[0]).at[1].set(c_ctx)
    mods = _modulation(c8, ada_w, ada_b)[:, :2].reshape(L, 2, 6, D)

    n_rows = -(-(T * TOP_K + (T // tm) * N_EXPERTS * (RUN_ALIGN - 1) + N_EXPERTS * (bm - 1)) // bm) * bm
    na_tabs = _na_bias_tables(na_rpb)
    n_exp = exp_w_gu.shape[1]
    w_gu, w_down = exp_w_gu, exp_w_down
    b_gu, b_down = exp_b_gu.reshape(L, n_exp, 1, -1), exp_b_down.reshape(L, n_exp, 1, -1)
    tq, tk = ts["tq"], ts["tk"]
    xs = jnp.concatenate([x[0], ctx[0]], axis=0)
    for l in range(L):
        lw = _layer_weights(l, p)
        mod = mods[l]
        (naq, nak, nav, gq, gk, gv, mq, mk, mv, lx, lg) = _premix(xs, mod, S, tm, consts, lw)
        oa = _neighbourhood(naq, nak, nav, na_tabs[l], S)
        ob_lat = _flash(gq, gk, gv, tq, tk, S // tq, 0, T // tk, 0)
        ob_ctx = _flash(gq, gk, gv, C, C, 1, S // C, 1, S // C)
        oc_lat = _flash(mq, mk, mv, tq, tk, S // tq, 0, T // tk, 0)
        oc_ctx = _flash(mq, mk, mv, C, C, 1, S // C, 1, S // C)
        yf = _lru_scan(lx, lw, S, ts["tc"], False)
        yb = _lru_scan(lx, lw, S, ts["tc"], True)
        x1, tok, slot, gate, cnt = _merge(xs, mod, oa, ob_lat, ob_ctx, oc_lat, oc_ctx, yf, yb, lg,
                                          S, tm, consts, lw)
        meta, zmeta, blk_expert, n_used = _routing_meta(cnt, bm, n_rows)
        xin = _dispatch(tok, slot, meta, zmeta, n_rows, tm, bm)
        y = _experts(xin, blk_expert, n_used, l, w_gu, b_gu, w_down, b_down, bm)
        xs = _combine(y, slot, gate, meta, x1, mod, S, tm)
    return xs[:S][None]
```

```python
import functools

import numpy as np
import jax
import jax.numpy as jnp
from jax import lax
from jax.experimental import pallas as pl
from jax.experimental.pallas import tpu as pltpu

F32 = jnp.float32
BF16 = jnp.bfloat16
HIGHEST = lax.Precision.HIGHEST

GRID_W = 64
HEAD_DIM = 64
N_HEADS = 4
GROUP_W = 256
GQA_KV_HEADS = 2
NA_KH = 8
NA_KW = 16
MLA_NOPE = 64
MLA_ROPE = 32
MLA_PAD = 128
LRU_C = 8.0
N_EXPERTS = 32
TOP_K = 4
SWIGLU_LIMIT = 7.0
SWIGLU_ALPHA = 1.702
ROPE_THETA = 10000.0
EPS = 1e-6
NEG = -1e30
LANES = 128
FLASH_ROW_BLOCK = 512
LOG2E = 1.4426950408889634
RUN_ALIGN = 8
CHUNK_ROWS = (32, 16, 8)
assert all(a == 2 * b for a, b in zip(CHUNK_ROWS, CHUNK_ROWS[1:])) and CHUNK_ROWS[-1] == RUN_ALIGN
VMEM_LIMIT = 56 * 1024 * 1024

_C_NAQ, _C_NAK, _C_NAV = 0, 256, 512
_C_GQ, _C_GK, _C_GV = 768, 1024, 1152
_C_MCQ, _C_MCKV = 1280, 1536
_C_LX, _C_LG = 1664, 1920
_C_MKR = 2176
IN_COLS_R = 2304


def _params(n_axes, vmem=VMEM_LIMIT):
    return pltpu.CompilerParams(dimension_semantics=("arbitrary",) * n_axes,
                                vmem_limit_bytes=vmem)


def _full(shape):
    n = len(shape)
    return pl.BlockSpec(shape, lambda *_: (0,) * n)


def _mod_kernel(c_ref, w_ref, b_ref, o_ref):
    cc = c_ref[...]
    s = cc * jax.nn.sigmoid(cc)
    o_ref[0] = jnp.dot(s, w_ref[0], preferred_element_type=F32, precision=HIGHEST) + b_ref[0]


def _modulation(c8, ada_w, ada_b):
    L, D, D6 = ada_w.shape
    tn = D6 // 6
    return pl.pallas_call(
        _mod_kernel,
        grid=(L, D6 // tn),
        in_specs=[pl.BlockSpec((8, D), lambda l, j: (0, 0)),
                  pl.BlockSpec((1, D, tn), lambda l, j: (l, 0, j)),
                  pl.BlockSpec((1, 1, tn), lambda l, j: (l, 0, j))],
        out_specs=pl.BlockSpec((1, 8, tn), lambda l, j: (l, 0, j)),
        out_shape=jax.ShapeDtypeStruct((L, 8, D6), F32),
        compiler_params=_params(2),
        name="adaln_mod",
    )(c8, ada_w, ada_b.reshape(L, 1, D6))


def _rms(z):
    return z * lax.rsqrt(jnp.mean(z * z, axis=-1, keepdims=True) + EPS)


def _group_rms(z, gmat, inv_n):
    zz = z * z
    hi = zz.astype(BF16)
    lo = (zz - hi.astype(F32)).astype(BF16)
    ss = (jnp.dot(hi, gmat, preferred_element_type=F32)
          + jnp.dot(lo, gmat, preferred_element_type=F32))
    return z * lax.rsqrt(ss * inv_n + EPS)


def _rope(z, cos, sin_lo, sin_hi, half):
    w = z.shape[-1]
    return z * cos + pltpu.roll(z, half, 1) * sin_lo + pltpu.roll(z, w - half, 1) * sin_hi


def _tile4(t):
    return jnp.concatenate([t, t, t, t], axis=-1)


def _rope_tile(row_ref, col_ref, width):
    rt = row_ref[0]
    rows = jnp.concatenate([jnp.broadcast_to(rt[g:g + 1], (GRID_W, rt.shape[1]))
                            for g in range(rt.shape[0])], axis=0)
    tab = rows + col_ref[0]
    return tab[:, :width], tab[:, width:2 * width], tab[:, 2 * width:]


def _premix_kernel(x_ref, mod_ref, g1_ref, win_ref, bd_ref, gm_ref,
                   rq_ref, cq_ref, rm_ref, cm_ref,
                   naqn_ref, nakn_ref, gqn_ref, gkn_ref, qag_ref, kvag_ref,
                   mqn_ref, mknn_ref, mknr_ref, invn_ref, wuq_ref, wukvk_ref, wukvv_ref,
                   naq_o, nak_o, nav_o, gq_o, gk_o, gv_o, mq_o, mk_o, mv_o, lx_o, lg_o):
    tm = x_ref.shape[0]
    xt = x_ref[...]
    sh1 = mod_ref[0, 0:1, :]
    sc1 = mod_ref[0, 1:2, :]
    h = _rms(xt) * g1_ref[...] * (1.0 + sc1) + sh1
    u = jnp.dot(h.astype(BF16), win_ref[...], preferred_element_type=F32)

    bd = bd_ref[...]
    inv64 = 1.0 / HEAD_DIM
    ones64 = jnp.ones((tm, HEAD_DIM), F32)

    naq_o[...] = (_group_rms(u[:, _C_NAQ:_C_NAQ + 256], bd, inv64) * naqn_ref[...]).astype(BF16)
    nak_o[...] = (_group_rms(u[:, _C_NAK:_C_NAK + 256], bd, inv64) * nakn_ref[...]).astype(BF16)
    nav_o[...] = u[:, _C_NAV:_C_NAV + 256].astype(BF16)

    cq, slq, shq = _rope_tile(rq_ref, cq_ref, 256)
    gq = _rope(_group_rms(u[:, _C_GQ:_C_GQ + 256], bd, inv64) * gqn_ref[...], cq, slq, shq, 16)
    for hh in range(N_HEADS):
        gq_o[hh] = gq[:, 64 * hh:64 * hh + 64].astype(BF16)
    gk = _rope(_group_rms(u[:, _C_GK:_C_GK + 128], bd[:128, :128], inv64) * gkn_ref[...],
               cq[:, :128], slq[:, :128], shq[:, :128], 16)
    gv = u[:, _C_GV:_C_GV + 128]
    for hh in range(GQA_KV_HEADS):
        gk_o[hh] = gk[:, 64 * hh:64 * hh + 64].astype(BF16)
        gv_o[hh] = jnp.concatenate([gv[:, 64 * hh:64 * hh + 64], ones64], axis=-1).astype(BF16)

    gm = gm_ref[...]
    invn = invn_ref[...]
    cm, slm, shm = _rope_tile(rm_ref, cm_ref, MLA_PAD)
    cqn =_rms(u[:, _C_MCQ:_C_MCQ + 256]) * qag_ref[...]
    mq = jnp.dot(cqn.astype(BF16), wuq_ref[...], preferred_element_type=F32)
    mq = _group_rms(mq, gm, invn) * mqn_ref[...]
    mq = _rope(mq, _tile4(cm), _tile4(slm), _tile4(shm), 8)
    ckvn = (_rms(u[:, _C_MCKV:_C_MCKV + 128]) * kvag_ref[...]).astype(BF16)
    mkn = jnp.dot(ckvn, wukvk_ref[...], preferred_element_type=F32)
    mkn = _group_rms(mkn, gm, invn) * mknn_ref[...]
    mvv = jnp.dot(ckvn, wukvv_ref[...], preferred_element_type=F32)
    kr = _group_rms(u[:, _C_MKR:_C_MKR + 128], gm[:128, :128], invn[:, :128]) * mknr_ref[...]
    kr = _rope(kr, cm, slm, shm, 8)
    for hh in range(N_HEADS):
        mq_o[hh] = mq[:, 128 * hh:128 * hh + 128].astype(BF16)
        mk_o[hh] = (mkn[:, 128 * hh:128 * hh + 128] + kr).astype(BF16)
        mv_o[hh] = jnp.concatenate([mvv[:, 64 * hh:64 * hh + 64], ones64], axis=-1).astype(BF16)

    lx_o[...] = u[:, _C_LX:_C_LX + 256]
    g = u[:, _C_LG:_C_LG + 256]
    lg_o[...] = 0.5 * g * (1.0 + jnp.tanh(0.7978845608028654 * (g + 0.044715 * g * g * g)))


def _premix(x, mod, S, tm, consts, lw):
    T, D = x.shape
    nL = S // tm
    row = lambda w: pl.BlockSpec((tm, w), lambda i: (i, 0))
    hm = lambda hn, w: pl.BlockSpec((hn, tm, w), lambda i: (0, i, 0))
    in_specs = [
        row(D),
        pl.BlockSpec((1, 6, D), lambda i: (jnp.where(i >= nL, 1, 0), 0, 0)),
        _full((1, D)), _full((D, IN_COLS_R)), _full((256, 256)), _full((512, 512)),
        pl.BlockSpec((1, tm // GRID_W, 768), lambda i: (i, 0, 0)),
        pl.BlockSpec((1, tm, 768), lambda i: (jnp.where(i >= nL, 1, 0), 0, 0)),
        pl.BlockSpec((1, tm // GRID_W, 3 * MLA_PAD), lambda i: (i, 0, 0)),
        pl.BlockSpec((1, tm, 3 * MLA_PAD), lambda i: (jnp.where(i >= nL, 1, 0), 0, 0)),
        _full((1, 256)), _full((1, 256)), _full((1, 256)), _full((1, 128)),
        _full((1, 256)), _full((1, 128)),
        _full((1, 512)), _full((1, 512)), _full((1, 128)), _full((1, 512)),
        _full((256, 512)), _full((128, 512)), _full((128, 256)),
    ]
    out_specs = [row(256), row(256), row(256),
                 hm(4, 64), hm(2, 64), hm(2, 128),
                 hm(4, 128), hm(4, 128), hm(4, 128),
                 row(256), row(256)]
    sds = jax.ShapeDtypeStruct
    out_shape = [sds((T, 256), BF16), sds((T, 256), BF16), sds((T, 256), BF16),
                 sds((4, T, 64), BF16), sds((2, T, 64), BF16), sds((2, T, 128), BF16),
                 sds((4, T, 128), BF16), sds((4, T, 128), BF16), sds((4, T, 128), BF16),
                 sds((T, 256), F32), sds((T, 256), F32)]
    return pl.pallas_call(
        _premix_kernel, grid=(T // tm,), in_specs=in_specs, out_specs=out_specs,
        out_shape=out_shape, compiler_params=_params(1), name="premix",
    )(x, mod, lw["g1"], lw["w_in"], consts["bd64"], consts["gm"],
      consts["rope_q_row"], consts["rope_q_col"], consts["rope_m_row"], consts["rope_m_col"],
      lw["na_qn"], lw["na_kn"], lw["gqa_qn"], lw["gqa_kn"], lw["qa_g"], lw["kva_g"],
      lw["mla_qn"], lw["mla_knn"], lw["mla_knr"], consts["invn"],
      lw["wuq"], lw["wukv_k"], lw["wukv_v"])


def _flash_kernel(q_ref, k_ref, v_ref, o_ref, m_scr, acc_scr, *, group, rb):
    j = pl.program_id(1)
    n_kv = k_ref.shape[0]
    tq = q_ref.shape[1]

    @pl.when(j == 0)
    def _():
        m_scr[...] = jnp.full(m_scr.shape, NEG, F32)
        acc_scr[...] = jnp.zeros(acc_scr.shape, F32)

    chains = [(hq, slice(r0, r0 + rb)) for hq in range(n_kv * group) for r0 in range(0, tq, rb)]

    def scores(chain):
        hq, rows = chain
        return lax.dot_general(q_ref[hq, rows, :], k_ref[hq // group], (((1,), (1,)), ((), ())),
                               preferred_element_type=F32)

    s = scores(chains[0])
    for n, (hq, rows) in enumerate(chains):
        s_next = scores(chains[n + 1]) if n + 1 < len(chains) else None
        m_old = m_scr[hq, rows, :]
        m_new = jnp.maximum(m_old, jnp.max(s, axis=-1, keepdims=True))
        alpha = jnp.exp2(m_old - m_new)
        p = jnp.exp2(s - m_new).astype(BF16)
        acc_scr[hq, rows, :] = (alpha * acc_scr[hq, rows, :]
                                + jnp.dot(p, v_ref[hq // group], preferred_element_type=F32))
        m_scr[hq, rows, :] = m_new
        s = s_next

    @pl.when(j == pl.num_programs(1) - 1)
    def _():
        outs = []
        for hq in range(n_kv * group):
            a = acc_scr[hq]
            outs.append(a[:, :HEAD_DIM] / a[:, HEAD_DIM:HEAD_DIM + 1])
        o_ref[...] = jnp.concatenate(outs, axis=-1)


def _flash(q, k, v, tq, tk, n_q, q_blk0, n_kv, kv_blk0):
    hq, _, d = q.shape
    hk = k.shape[0]
    group = hq // hk
    return pl.pallas_call(
        functools.partial(_flash_kernel, group=group, rb=min(FLASH_ROW_BLOCK, tq)),
        grid=(n_q, n_kv),
        in_specs=[pl.BlockSpec((hq, tq, d), lambda i, j: (0, q_blk0 + i, 0)),
                  pl.BlockSpec((hk, tk, d), lambda i, j: (0, kv_blk0 + j, 0)),
                  pl.BlockSpec((hk, tk, 128), lambda i, j: (0, kv_blk0 + j, 0))],
        out_specs=pl.BlockSpec((tq, hq * HEAD_DIM), lambda i, j: (i, 0)),
        out_shape=jax.ShapeDtypeStruct((n_q * tq, hq * HEAD_DIM), F32),
        scratch_shapes=[pltpu.VMEM((hq, tq, 1), F32), pltpu.VMEM((hq, tq, 128), F32)],
        compiler_params=_params(2), name="flash_attn",
    )(q, k, v)


def _na_kernel(q_ref, k_ref, v_ref, tab_ref, o_ref, *, rows, n_ctx):
    i = pl.program_id(0)
    s_lat = rows * GRID_W
    kc = k_ref[pl.ds(s_lat, n_ctx), :]
    vc = v_ref[pl.ds(s_lat, n_ctx), :]
    nq = N_HEADS * GRID_W
    same_head = (lax.broadcasted_iota(jnp.int32, (nq, GROUP_W), 0) // GRID_W
                 == lax.broadcasted_iota(jnp.int32, (nq, GROUP_W), 1) // HEAD_DIM)
    q = q_ref[...].astype(F32)
    q_bd = jnp.where(same_head, jnp.concatenate([q] * N_HEADS, axis=0), 0.0).astype(BF16)
    nt = (((1,), (1,)), ((), ()))
    s_cx = lax.dot_general(q_bd, kc, nt, preferred_element_type=F32)

    def heads_out(res):
        res = jnp.where(same_head, res, 0.0)
        out = res[0:GRID_W]
        for h in range(1, N_HEADS):
            out = out + res[GRID_W * h:GRID_W * (h + 1)]
        return out

    @pl.when(i < rows)
    def _():
        r0 = jnp.clip(i - NA_KH // 2, 0, rows - NA_KH)
        start = pl.multiple_of(r0 * GRID_W, GRID_W)
        ks = k_ref[pl.ds(start, NA_KH * GRID_W), :]
        vs = v_ref[pl.ds(start, NA_KH * GRID_W), :]
        s_nb = (lax.dot_general(q_bd, ks, nt, preferred_element_type=F32)
                + tab_ref[i - r0].reshape(nq, NA_KH * GRID_W))
        m = jnp.maximum(jnp.max(s_nb, axis=-1, keepdims=True), jnp.max(s_cx, axis=-1, keepdims=True))
        p_nb = jnp.exp(s_nb - m)
        p_cx = jnp.exp(s_cx - m)
        l = jnp.sum(p_nb, axis=-1, keepdims=True) + jnp.sum(p_cx, axis=-1, keepdims=True)
        res = (jnp.dot(p_nb.astype(BF16), vs, preferred_element_type=F32)
               + jnp.dot(p_cx.astype(BF16), vc, preferred_element_type=F32))
        o_ref[...] = heads_out(res / l)

    @pl.when(i >= rows)
    def _():
        m = jnp.max(s_cx, axis=-1, keepdims=True)
        p_cx = jnp.exp(s_cx - m)
        l = jnp.sum(p_cx, axis=-1, keepdims=True)
        res = jnp.dot(p_cx.astype(BF16), vc, preferred_element_type=F32)
        o_ref[...] = heads_out(res / l)


def _neighbourhood(q, k, v, tab, S):
    T = q.shape[0]
    rows = S // GRID_W
    n_ctx = T - S
    resident = lambda shape: pl.BlockSpec(shape, lambda i: (0,) * len(shape),
                                          pipeline_mode=pl.Buffered(1))
    return pl.pallas_call(
        functools.partial(_na_kernel, rows=rows, n_ctx=n_ctx),
        grid=(T // GRID_W,),
        in_specs=[pl.BlockSpec((GRID_W, 256), lambda i: (i, 0)),
                  resident((T, 256)), resident((T, 256)),
                  resident((NA_KH, N_HEADS, GRID_W, NA_KH * GRID_W))],
        out_specs=pl.BlockSpec((GRID_W, 256), lambda i: (i, 0)),
        out_shape=jax.ShapeDtypeStruct((T, 256), F32),
        compiler_params=_params(1), name="nbr_attn",
    )(q, k, v, tab)


def _lru_kernel(x_ref, xp_ref, xn_ref, cw_ref, cb_ref, wa_ref, ba_ref, wi_ref, bi_ref,
                lam_ref, y_ref, h_scr, *, reverse, n_lat, n_ctx):
    j = pl.program_id(0)
    tc = x_ref.shape[0]
    n_all = n_lat + n_ctx
    if reverse:
        chunk = jnp.where(j < n_ctx, n_all - 1 - j, n_all - 1 - j)
    else:
        chunk = jnp.where(j < n_ctx, n_lat + j, j - n_ctx)
    has_prev = jnp.logical_and(chunk != 0, chunk != n_lat)
    has_next = jnp.logical_and(chunk != n_lat - 1, chunk != n_all - 1)

    @pl.when(j == 0)
    def _():
        h_scr[...] = jnp.zeros(h_scr.shape, F32)

    xp = jnp.where(has_prev, xp_ref[...], 0.0)
    xn = jnp.where(has_next, xn_ref[...], 0.0)
    xe = jnp.concatenate([xp, x_ref[...], xn], axis=0)
    ne = tc + 16
    cw = cw_ref[...]
    xc = (cw[0:1] * pltpu.roll(xe, 1, 0)[8:8 + tc]
          + cw[1:2] * xe[8:8 + tc]
          + cw[2:3] * pltpu.roll(xe, ne - 1, 0)[8:8 + tc]
          + cw[3:4] * pltpu.roll(xe, ne - 2, 0)[8:8 + tc]
          + cb_ref[...])

    r = jax.nn.sigmoid(jnp.dot(xc, wa_ref[0], preferred_element_type=F32, precision=HIGHEST)
                       + ba_ref[0])
    gi = jax.nn.sigmoid(jnp.dot(xc, wi_ref[0], preferred_element_type=F32, precision=HIGHEST)
                        + bi_ref[0])
    z = -lam_ref[0]
    softplus = jnp.maximum(z, 0.0) + jnp.log1p(jnp.exp(-jnp.abs(z)))
    log_a = -LRU_C * r * softplus
    a = jnp.exp(log_a)
    b = jnp.sqrt(-jnp.tanh(log_a) * (a * a + 1.0)) * (gi * xc)

    rowi = lax.broadcasted_iota(jnp.int32, (tc, 1), 0)
    s = 1
    while s < tc:
        if reverse:
            keep = rowi < tc - s
            a_s = jnp.where(keep, pltpu.roll(a, tc - s, 0), 1.0)
            b_s = jnp.where(keep, pltpu.roll(b, tc - s, 0), 0.0)
        else:
            keep = rowi >= s
            a_s = jnp.where(keep, pltpu.roll(a, s, 0), 1.0)
            b_s = jnp.where(keep, pltpu.roll(b, s, 0), 0.0)
        b = a * b_s + b
        a = a * a_s
        s *= 2
    hcur = b + a * h_scr[...]
    y_ref[...] = hcur
    h_scr[...] = hcur[0:1] if reverse else hcur[tc - 1:tc]


def _lru_scan(lx, lw, S, tc, reverse):
    T = lx.shape[0]
    n_lat, n_ctx = S // tc, (T - S) // tc
    n_all = n_lat + n_ctx
    d = 1 if reverse else 0
    t8 = tc // 8

    def chunk_of(j):
        if reverse:
            return n_all - 1 - j
        return jnp.where(j < n_ctx, n_lat + j, j - n_ctx)

    dspec = lambda shape: pl.BlockSpec((1,) + shape, lambda j: (d,) + (0,) * len(shape))
    return pl.pallas_call(
        functools.partial(_lru_kernel, reverse=reverse, n_lat=n_lat, n_ctx=n_ctx),
        grid=(n_all,),
        in_specs=[pl.BlockSpec((tc, 256), lambda j: (chunk_of(j), 0)),
                  pl.BlockSpec((8, 256), lambda j: (jnp.maximum(chunk_of(j) * t8 - 1, 0), 0)),
                  pl.BlockSpec((8, 256),
                               lambda j: (jnp.minimum((chunk_of(j) + 1) * t8, T // 8 - 1), 0)),
                  _full((4, 256)), _full((1, 256)),
                  dspec((256, 256)), dspec((1, 256)), dspec((256, 256)), dspec((1, 256)),
                  dspec((1, 256))],
        out_specs=pl.BlockSpec((tc, 256), lambda j: (chunk_of(j), 0)),
        out_shape=jax.ShapeDtypeStruct((T, 256), F32),
        scratch_shapes=[pltpu.VMEM((1, 256), F32)],
        compiler_params=_params(1), name="rglru_bwd" if reverse else "rglru_fwd",
    )(lx, lx, lx, lw["conv_w"], lw["conv_b"], lw["wa"], lw["ba"], lw["wi"], lw["bi"], lw["lam"])


def _merge_kernel(x_ref, mod_ref, oa_ref, obl_ref, obc_ref, ocl_ref, occ_ref, yf_ref, yb_ref,
                  lg_ref, gg_ref, wout_ref, g2_ref, rw_ref, rb_ref, lt_ref, ut_ref,
                  x1_o, tok_o, slot_o, gate_o, cnt_o, *, n_lat_tiles):
    i = pl.program_id(0)
    tm = x_ref.shape[0]
    is_ctx = i >= n_lat_tiles

    gg = gg_ref[...]
    od = (yf_ref[...] + yb_ref[...]) * lg_ref[...]
    ob = jnp.where(is_ctx, obc_ref[...], obl_ref[...])
    oc = jnp.where(is_ctx, occ_ref[...], ocl_ref[...])
    parts = [oa_ref[...], ob, oc, od]
    ycat = jnp.concatenate(
        [(_rms(p) * gg[:, GROUP_W * n:GROUP_W * (n + 1)]).astype(BF16) for n, p in enumerate(parts)],
        axis=-1)
    y = jnp.dot(ycat, wout_ref[...], preferred_element_type=F32)
    gt1 = mod_ref[0, 2:3, :]
    sh2 = mod_ref[0, 3:4, :]
    sc2 = mod_ref[0, 4:5, :]
    x1 = x_ref[...] + gt1 * y
    x1_o[...] = x1
    tok = _rms(x1) * g2_ref[...] * (1.0 + sc2) + sh2
    tok_o[...] = tok.astype(BF16)

    rw = rw_ref[...]
    t_hi = tok.astype(BF16)
    t_lo = (tok - t_hi.astype(F32)).astype(BF16)
    r_hi = rw.astype(BF16)
    r_lo = (rw - r_hi.astype(F32)).astype(BF16)
    logits = (jnp.dot(t_hi, r_hi, preferred_element_type=F32)
              + jnp.dot(t_hi, r_lo, preferred_element_type=F32)
              + jnp.dot(t_lo, r_hi, preferred_element_type=F32)) + rb_ref[...]
    lane = lax.broadcasted_iota(jnp.int32, (tm, LANES), 1)
    v_acc = jnp.zeros((tm, LANES), F32)
    onehots = []
    v0 = None
    work = logits
    for k in range(TOP_K):
        vk = jnp.max(work, axis=-1, keepdims=True)
        ek = jnp.min(jnp.where(work == vk, lane, LANES), axis=-1, keepdims=True)
        sel = lane == ek
        onehots.append(sel)
        work = jnp.where(sel, NEG * 2.0, work)
        if k == 0:
            v0 = vk
        v_acc = jnp.where(lane == k, jnp.exp(vk - v0), v_acc)
    gate_o[...] = v_acc / jnp.sum(v_acc, axis=-1, keepdims=True)

    oh = [jnp.where(o, 1.0, 0.0) for o in onehots]
    oh_all = oh[0] + oh[1] + oh[2] + oh[3]
    cnt = jnp.sum(oh_all, axis=0, keepdims=True)
    run = jnp.floor((cnt + (RUN_ALIGN - 1.0)) * (1.0 / RUN_ALIGN)) * RUN_ALIGN
    run_start = jnp.dot(jnp.broadcast_to(run, (8, LANES)).astype(BF16), ut_ref[...],
                        preferred_element_type=F32)[0:1]
    excl = jnp.dot(lt_ref[...], oh_all.astype(BF16), preferred_element_type=F32) + run_start
    slot = jnp.full((tm, LANES), -1.0, F32)
    for k in range(TOP_K):
        sk = jnp.sum(oh[k] * excl, axis=-1, keepdims=True)
        slot = jnp.where(lane == k, sk, slot)
    slot_o[...] = slot.astype(jnp.int32)
    cnt_o[0] = jnp.broadcast_to(cnt, (8, LANES))


def _merge(x, mod, oa, ob_lat, ob_ctx, oc_lat, oc_ctx, yf, yb, lg, S, tm, consts, lw):
    T, D = x.shape
    nL = S // tm
    nt = T // tm
    row = lambda w: pl.BlockSpec((tm, w), lambda i: (i, 0))
    lat = pl.BlockSpec((tm, 256), lambda i: (jnp.minimum(i, nL - 1), 0))
    ctx = pl.BlockSpec((tm, 256), lambda i: (jnp.maximum(i - nL, 0), 0))
    sds = jax.ShapeDtypeStruct
    return pl.pallas_call(
        functools.partial(_merge_kernel, n_lat_tiles=nL), grid=(nt,),
        in_specs=[row(D), pl.BlockSpec((1, 6, D), lambda i: (jnp.where(i >= nL, 1, 0), 0, 0)),
                  row(256), lat, ctx, lat, ctx, row(256), row(256), row(256),
                  _full((1, D)), _full((D, D)), _full((1, D)), _full((D, LANES)),
                  _full((1, LANES)), _full((tm, tm)), _full((LANES, LANES))],
        out_specs=[row(D), row(D), row(LANES), row(LANES),
                   pl.BlockSpec((1, 8, LANES), lambda i: (i, 0, 0))],
        out_shape=[sds((T, D), F32), sds((T, D), BF16), sds((T, LANES), jnp.int32),
                   sds((T, LANES), F32), sds((nt, 8, LANES), F32)],
        compiler_params=_params(1), name="merge_router",
    )(x, mod, oa, ob_lat, ob_ctx, oc_lat, oc_ctx, yf, yb, lg, lw["grp_g"], lw["w_out"], lw["g2"],
      lw["router_w"], lw["router_b"], consts["lt"], consts["ut"])


def _routing_meta(cnt, bm, n_rows):
    c = cnt[:, 0, :N_EXPERTS].astype(jnp.int32)
    nt = c.shape[0]
    run = (c + RUN_ALIGN - 1) // RUN_ALIGN * RUN_ALIGN
    tile_off = jnp.cumsum(run, axis=0) - run
    tot = jnp.sum(run, axis=0)
    padded = (tot + bm - 1) // bm * bm
    seg_end = jnp.cumsum(padded)
    seg_start = seg_end - padded
    g_start = seg_start[None, :] + tile_off
    l_start = jnp.cumsum(run, axis=1) - run
    per_size, done = [], jnp.zeros_like(run)
    for size in CHUNK_ROWS:
        n = (run - done) // size
        per_size.append(jnp.sum(n, axis=1, keepdims=True))
        done = done + n * size
    n_tot = jnp.sum(run, axis=1, keepdims=True) // RUN_ALIGN
    meta = jnp.concatenate(
        [g_start, l_start, run] + per_size
        + [n_tot, jnp.zeros((nt, LANES - 3 * N_EXPERTS - len(CHUNK_ROWS) - 1), jnp.int32)], axis=1)
    n_zero = (padded - tot) // RUN_ALIGN
    zmeta = jnp.concatenate([seg_start + tot, n_zero, jnp.sum(n_zero)[None], seg_end[-1:] // bm,
                             jnp.zeros((LANES - 2 * N_EXPERTS - 2,), jnp.int32)])
    blk_row = jnp.arange(n_rows // bm, dtype=jnp.int32) * bm
    blk_expert = jnp.minimum(jnp.sum((seg_end[None, :] <= blk_row[:, None]).astype(jnp.int32), axis=1),
                             N_EXPERTS - 1)
    n_used = seg_end[-1:] // bm
    return (meta.reshape(nt, 1, LANES).astype(jnp.int32), zmeta.reshape(1, 1, LANES).astype(jnp.int32),
            blk_expert.astype(jnp.int32), n_used.astype(jnp.int32))


def _run_copies(meta_ref, make_copy):
    def per_expert(e, c):
        g = meta_ref[0, 0, e]
        l = meta_ref[0, 0, N_EXPERTS + e]
        rows = meta_ref[0, 0, 2 * N_EXPERTS + e]
        big = CHUNK_ROWS[0]

        def issue(cc, c2):
            off = big * cc
            make_copy(pl.multiple_of(g + off, RUN_ALIGN), pl.multiple_of(l + off, RUN_ALIGN),
                      big).start()
            return c2

        c = lax.fori_loop(0, rows // big, issue, c)
        done = rows // big * big
        for size in CHUNK_ROWS[1:]:
            use = (rows - done) >= size

            @pl.when(use)
            def _(size=size, done=done):
                make_copy(pl.multiple_of(g + done, RUN_ALIGN), pl.multiple_of(l + done, RUN_ALIGN),
                          size).start()

            done = done + jnp.where(use, size, 0)
        return c

    lax.fori_loop(0, N_EXPERTS, per_expert, 0)


def _drain(meta_ref, make_copy):
    for n, size in enumerate(CHUNK_ROWS):
        def body(_, c, size=size):
            make_copy(0, 0, size).wait()
            return c

        lax.fori_loop(0, meta_ref[0, 0, 3 * N_EXPERTS + n], body, 0)


def _dispatch_kernel(meta_ref, prev_meta_ref, zmeta_ref, slot_ref, tok_ref, xin_ref,
                     xs_scr, z_scr, sem):
    i = pl.program_id(0)
    tm = tok_ref.shape[0]
    n_slot = xs_scr.shape[1]
    buf = lax.rem(i, 2)
    slot_t = jnp.transpose(slot_ref[...].astype(F32))
    srow = lax.broadcasted_iota(jnp.int32, (n_slot, tm), 0).astype(F32)
    perm = jnp.zeros((n_slot, tm), F32)
    for k in range(TOP_K):
        perm = perm + jnp.where(srow == slot_t[k:k + 1, :], 1.0, 0.0)
    xs_scr[buf] = jnp.dot(perm.astype(BF16), tok_ref[...], preferred_element_type=F32)

    def to_global(g_row, l_row, size):
        return pltpu.make_async_copy(xs_scr.at[buf, pl.ds(l_row, size)],
                                     xin_ref.at[pl.ds(g_row, size)], sem)

    @pl.when(i > 0)
    def _():
        _drain(prev_meta_ref, to_global)

    _run_copies(meta_ref, to_global)

    @pl.when(i == pl.num_programs(0) - 1)
    def _():
        _drain(meta_ref, to_global)
        z_scr[...] = jnp.zeros(z_scr.shape, F32)

        def zero_copy(g_row):
            return pltpu.make_async_copy(z_scr.at[pl.ds(0, RUN_ALIGN)],
                                         xin_ref.at[pl.ds(g_row, RUN_ALIGN)], sem)

        def per_expert(e, c):
            g = zmeta_ref[0, 0, e]

            def issue(cc, c2):
                zero_copy(pl.multiple_of(g + RUN_ALIGN * cc, RUN_ALIGN)).start()
                return c2

            return lax.fori_loop(0, zmeta_ref[0, 0, N_EXPERTS + e], issue, c)

        lax.fori_loop(0, N_EXPERTS, per_expert, 0)

        def drain(_, c):
            zero_copy(0).wait()
            return c

        lax.fori_loop(0, zmeta_ref[0, 0, 2 * N_EXPERTS], drain, 0)

        bm = z_scr.shape[0]

        def block_copy(b):
            return pltpu.make_async_copy(z_scr, xin_ref.at[pl.ds(pl.multiple_of(b * bm, bm), bm)], sem)

        def issue_block(b, c):
            block_copy(b).start()
            return c

        def drain_block(b, c):
            block_copy(b).wait()
            return c

        n_used = zmeta_ref[0, 0, 2 * N_EXPERTS + 1]
        lax.fori_loop(n_used, xin_ref.shape[0] // bm, issue_block, 0)
        lax.fori_loop(n_used, xin_ref.shape[0] // bm, drain_block, 0)


def _dispatch(tok, slot, meta, zmeta, n_rows, tm, bm):
    T, D = tok.shape
    n_slot = TOP_K * tm + N_EXPERTS * RUN_ALIGN
    smem = lambda idx: pl.BlockSpec((1, 1, LANES), idx, memory_space=pltpu.SMEM)
    return pl.pallas_call(
        _dispatch_kernel,
        grid=(T // tm,),
        in_specs=[smem(lambda i: (i, 0, 0)), smem(lambda i: (jnp.maximum(i - 1, 0), 0, 0)),
                  smem(lambda i: (0, 0, 0)),
                  pl.BlockSpec((tm, LANES), lambda i: (i, 0)),
                  pl.BlockSpec((tm, D), lambda i: (i, 0))],
        out_specs=pl.BlockSpec(memory_space=pl.ANY),
        out_shape=jax.ShapeDtypeStruct((n_rows, D), F32),
        scratch_shapes=[pltpu.VMEM((2, n_slot, D), F32), pltpu.VMEM((bm, D), F32),
                        pltpu.SemaphoreType.DMA(())],
        compiler_params=_params(1), name="moe_dispatch",
    )(meta, meta, zmeta, slot, tok)


def _expert_kernel(be_ref, nu_ref, x_ref, wgu_ref, bgu_ref, wd_ref, bd_ref, y_ref,
                   wgu_bf, wd_bf):
    i = pl.program_id(0)
    d_e = wd_ref.shape[1]

    @pl.when(i < nu_ref[0])
    def _():
        @pl.when(jnp.logical_or(i == 0, be_ref[i] != be_ref[jnp.maximum(i - 1, 0)]))
        def _():
            wgu_bf[...] = wgu_ref[0].astype(BF16)
            wd_bf[...] = wd_ref[0].astype(BF16)

        gu = jnp.dot(x_ref[...].astype(BF16), wgu_bf[...], preferred_element_type=F32) + bgu_ref[0]
        x_glu = jnp.minimum(gu[:, :d_e], SWIGLU_LIMIT)
        x_lin = jnp.clip(gu[:, d_e:], -SWIGLU_LIMIT, SWIGLU_LIMIT)
        act = x_glu * jax.nn.sigmoid(SWIGLU_ALPHA * x_glu) * (x_lin + 1.0)
        y_ref[...] = jnp.dot(act.astype(BF16), wd_bf[...], preferred_element_type=F32) + bd_ref[0]

    @pl.when(i >= nu_ref[0])
    def _():
        y_ref[...] = jnp.zeros(y_ref.shape, F32)


def _experts(xin, blk_expert, n_used, layer, w_gu, b_gu, w_down, b_down, bm):
    P, D = xin.shape
    d2 = w_gu.shape[-1]
    d_e = d2 // 2
    blk = lambda i, be, nu: jnp.minimum(i, nu[0] - 1)
    wspec = lambda r, c: pl.BlockSpec((None, 1, r, c),
                                      lambda i, be, nu: (layer, be[blk(i, be, nu)], 0, 0))
    grid_spec = pltpu.PrefetchScalarGridSpec(
        num_scalar_prefetch=2, grid=(P // bm,),
        in_specs=[pl.BlockSpec((bm, D), lambda i, be, nu: (blk(i, be, nu), 0)),
                  wspec(D, d2), wspec(1, d2), wspec(d_e, D), wspec(1, D)],
        out_specs=pl.BlockSpec((bm, D), lambda i, be, nu: (i, 0)),
        scratch_shapes=[pltpu.VMEM((D, d2), BF16), pltpu.VMEM((d_e, D), BF16)])
    return pl.pallas_call(
        _expert_kernel, grid_spec=grid_spec,
        out_shape=jax.ShapeDtypeStruct((P, D), F32),
        compiler_params=_params(1), name="moe_experts",
    )(blk_expert, n_used, xin, w_gu, b_gu, w_down, b_down)


def _combine_kernel(meta_ref, next_meta_ref, slot_ref, gate_ref, y_ref, x_ref, mod_ref, o_ref,
                    ybuf, sem):
    i = pl.program_id(0)
    tm = x_ref.shape[0]
    _, n_slot, d = ybuf.shape
    buf = lax.rem(i, 2)

    def fetch(m_ref, b):
        def to_local(g_row, l_row, size):
            return pltpu.make_async_copy(y_ref.at[pl.ds(g_row, size)],
                                         ybuf.at[b, pl.ds(l_row, size)], sem)

        _run_copies(m_ref, to_local)
        n_tot = m_ref[0, 0, 3 * N_EXPERTS + len(CHUNK_ROWS)]

        def zero_tail(cc, c):
            ybuf[b, pl.ds(pl.multiple_of(RUN_ALIGN * cc, RUN_ALIGN), RUN_ALIGN), :] = jnp.zeros(
                (RUN_ALIGN, d), F32)
            return c

        lax.fori_loop(n_tot, n_slot // RUN_ALIGN, zero_tail, 0)
        return to_local

    @pl.when(i == 0)
    def _():
        fetch(meta_ref, buf)

    def wait_shape(g_row, l_row, size):
        return pltpu.make_async_copy(y_ref.at[pl.ds(g_row, size)],
                                     ybuf.at[buf, pl.ds(l_row, size)], sem)

    _drain(meta_ref, wait_shape)

    @pl.when(i + 1 < pl.num_programs(0))
    def _():
        fetch(next_meta_ref, 1 - buf)

    slot = slot_ref[...]
    gate = gate_ref[...]
    col = lax.broadcasted_iota(jnp.int32, (tm, n_slot), 1)
    w = jnp.zeros((tm, n_slot), F32)
    for k in range(TOP_K):
        w = w + jnp.where(col == slot[:, k:k + 1], gate[:, k:k + 1], 0.0)
    w_hi = w.astype(BF16)
    w_lo = (w - w_hi.astype(F32)).astype(BF16)
    yb = ybuf[buf].astype(BF16)
    f = (jnp.dot(w_hi, yb, preferred_element_type=F32)
         + jnp.dot(w_lo, yb, preferred_element_type=F32))
    o_ref[...] = x_ref[...] + mod_ref[0, 5:6, :] * f


def _combine(y, slot, gate, meta, x1, mod, S, tm):
    T, D = x1.shape
    nL = S // tm
    n_slot = TOP_K * tm + N_EXPERTS * RUN_ALIGN
    return pl.pallas_call(
        _combine_kernel,
        grid=(T // tm,),
        in_specs=[pl.BlockSpec((1, 1, LANES), lambda i: (i, 0, 0), memory_space=pltpu.SMEM),
                  pl.BlockSpec((1, 1, LANES), lambda i: (jnp.minimum(i + 1, T // tm - 1), 0, 0),
                               memory_space=pltpu.SMEM),
                  pl.BlockSpec((tm, LANES), lambda i: (i, 0)),
                  pl.BlockSpec((tm, LANES), lambda i: (i, 0)),
                  pl.BlockSpec(memory_space=pl.ANY),
                  pl.BlockSpec((tm, D), lambda i: (i, 0)),
                  pl.BlockSpec((1, 6, D), lambda i: (jnp.where(i >= nL, 1, 0), 0, 0))],
        out_specs=pl.BlockSpec((tm, D), lambda i: (i, 0)),
        out_shape=jax.ShapeDtypeStruct((T, D), F32),
        scratch_shapes=[pltpu.VMEM((2, n_slot, D), F32), pltpu.SemaphoreType.DMA(())],
        compiler_params=_params(1), name="moe_combine",
    )(meta, meta, slot, gate, y, x1, mod)


def _rope_tables(S, T, tm, rot_dim, width, lane0, reps):
    h = rot_dim // 4
    ax = rot_dim // 2
    inv = ROPE_THETA ** (-jnp.arange(0, ax, 2, dtype=F32) / ax)

    def parts(n):
        ang = jnp.arange(n, dtype=F32)[:, None] * inv
        return jnp.cos(ang), jnp.sin(ang), jnp.zeros((n, h), F32)

    def place(blocks, fill):
        n = blocks[0].shape[0]
        tab = jnp.concatenate(
            [jnp.full((n, lane0), fill, F32)] + list(blocks)
            + [jnp.full((n, width - lane0 - rot_dim), fill, F32)], axis=-1)
        return jnp.tile(tab, (1, reps))

    n_rows = S // GRID_W
    c, s, z = parts(n_rows)
    row = jnp.concatenate([place([c, c, z, z], 0.0), place([z, s, z, z], 0.0),
                           place([-s, z, z, z], 0.0)], axis=-1)
    row = jnp.concatenate([row, jnp.zeros(((T - S) // GRID_W, row.shape[1]), F32)], axis=0)
    c, s, z = parts(GRID_W)
    col = jnp.concatenate([place([z, z, c, c], 1.0), place([z, z, z, s], 0.0),
                           place([z, z, -s, z], 0.0)], axis=-1)
    col = jnp.tile(col, (tm // GRID_W, 1))
    ident = jnp.concatenate([jnp.ones((tm, reps * width), F32),
                             jnp.zeros((tm, 2 * reps * width), F32)], axis=-1)
    return row.reshape(T // tm, tm // GRID_W, -1), jnp.stack([col, ident])


def _constants(S, T, tm):
    rope_q_row, rope_q_col = _rope_tables(S, T, tm, HEAD_DIM, HEAD_DIM, 0, N_HEADS)
    rope_m_row, rope_m_col = _rope_tables(S, T, tm, MLA_ROPE, MLA_PAD, MLA_NOPE, 1)
    bd64 = np.kron(np.eye(4, dtype=np.float32), np.ones((64, 64), np.float32))
    sizes = [MLA_NOPE, MLA_ROPE, MLA_PAD - MLA_NOPE - MLA_ROPE] * N_HEADS
    gid = np.repeat(np.arange(len(sizes)), sizes)
    gm = (gid[:, None] == gid[None, :]).astype(np.float32)
    invn = (1.0 / np.repeat(np.asarray(sizes, np.float32), sizes))[None, :]
    lt = np.tril(np.ones((tm, tm), np.float32), -1)
    ut = np.triu(np.ones((LANES, LANES), np.float32), 1)
    return dict(rope_q_row=rope_q_row, rope_q_col=rope_q_col,
                rope_m_row=rope_m_row, rope_m_col=rope_m_col,
                bd64=jnp.asarray(bd64, BF16), gm=jnp.asarray(gm, BF16),
                invn=jnp.asarray(invn), lt=jnp.asarray(lt, BF16), ut=jnp.asarray(ut, BF16))


def _na_bias_tables(rpb):
    qc = np.arange(GRID_W)[:, None]
    kc = np.arange(GRID_W)[None, :]
    c0 = np.clip(qc - NA_KW // 2, 0, GRID_W - NA_KW)
    valid = (kc >= c0) & (kc < c0 + NA_KW)
    sel = ((kc - qc + NA_KW - 1)[:, :, None] == np.arange(2 * NA_KW - 1)) & valid[:, :, None]
    toep = jnp.einsum("lhab,qkb->lhaqk", rpb, jnp.asarray(sel, F32), precision=HIGHEST)
    toep = jnp.where(jnp.asarray(valid)[None, None, None], toep, NEG)
    L, H = rpb.shape[:2]
    tabs = []
    for off in range(NA_KH):
        rows = toep[:, :, NA_KH - 1 - off:2 * NA_KH - 1 - off]
        tabs.append(rows.transpose(0, 1, 3, 2, 4).reshape(L, H, GRID_W, NA_KH * GRID_W))
    return jnp.stack(tabs, axis=1)


def _block_diag(w):
    n, bw, _ = w.shape
    eye = jnp.eye(n, dtype=w.dtype)
    return (eye[:, None, :, None] * w[:, :, None, :]).reshape(n * bw, n * bw)


def _layer_weights(l, p):
    D = p["w_in"].shape[1]
    w_in = p["w_in"][l]
    offs = np.cumsum([0, 256, 256, 256, 256, 128, 128, 256, 128, 32, 256, 256])
    seg = lambda n: w_in[:, offs[n]:offs[n + 1]]
    z = lambda n: jnp.zeros((D, n), F32)
    w_in_r = jnp.concatenate(
        [seg(0), seg(1), seg(2), seg(3), seg(4), seg(5), seg(6), seg(7), seg(9), seg(10),
         z(MLA_NOPE), seg(8), z(MLA_PAD - MLA_NOPE - MLA_ROPE)], axis=-1).astype(BF16)
    sc = HEAD_DIM ** -0.5
    sc_m = (MLA_NOPE + MLA_ROPE) ** -0.5
    t4 = lambda g: jnp.tile(g, N_HEADS)[None, :]
    wuq = p["mla_wuq"][l].reshape(-1, N_HEADS, MLA_NOPE + MLA_ROPE)
    wuq = jnp.pad(wuq, ((0, 0), (0, 0), (0, MLA_PAD - MLA_NOPE - MLA_ROPE)))
    wukv = p["mla_wukv"][l].reshape(-1, N_HEADS, MLA_NOPE + HEAD_DIM)
    wukv_k = jnp.pad(wukv[:, :, :MLA_NOPE], ((0, 0), (0, 0), (0, MLA_PAD - MLA_NOPE)))
    qn, kn = p["mla_qn"][l], p["mla_kn"][l]
    padq = jnp.pad(qn * (sc_m * LOG2E), (0, MLA_PAD - MLA_NOPE - MLA_ROPE))
    padkn = jnp.pad(kn[:MLA_NOPE], (0, MLA_PAD - MLA_NOPE))
    padkr = jnp.pad(kn[MLA_NOPE:], (MLA_NOPE, MLA_PAD - MLA_NOPE - MLA_ROPE))
    rw =jnp.pad(p["router_w"][l], ((0, 0), (0, LANES - N_EXPERTS)))
    rb = jnp.pad(p["router_b"][l], (0, LANES - N_EXPERTS), constant_values=NEG)[None, :]
    return dict(
        g1=p["norm1_g"][l][None, :], g2=p["norm2_g"][l][None, :], w_in=w_in_r,
        na_qn=t4(p["na_qn"][l] * sc), na_kn=t4(p["na_kn"][l]),
        gqa_qn=t4(p["gqa_qn"][l] * (sc * LOG2E)), gqa_kn=jnp.tile(p["gqa_kn"][l], GQA_KV_HEADS)[None, :],
        qa_g=p["mla_qa_g"][l][None, :], kva_g=p["mla_kva_g"][l][None, :],
        mla_qn=t4(padq), mla_knn=t4(padkn), mla_knr=padkr[None, :],
        wuq=wuq.reshape(-1, N_HEADS * MLA_PAD).astype(BF16),
        wukv_k=wukv_k.reshape(-1, N_HEADS * MLA_PAD).astype(BF16),
        wukv_v=wukv[:, :, MLA_NOPE:].reshape(-1, N_HEADS * HEAD_DIM).astype(BF16),
        conv_w=p["lru_conv_w"][l][:, 0, :], conv_b=p["lru_conv_b"][l][None, :],
        wa=jnp.stack([_block_diag(p["lru_wa"][l][d]) for d in range(2)]),
        wi=jnp.stack([_block_diag(p["lru_wi"][l][d]) for d in range(2)]),
        ba=p["lru_ba"][l][:, None, :], bi=p["lru_bi"][l][:, None, :],
        lam=p["lru_lam"][l][:, None, :],
        grp_g=p["grp_g"][l][None, :], w_out=p["w_out"][l].astype(BF16),
        router_w=rw, router_b=rb,
    )


def _tiles(S, C):
    tm = min(256, C)
    tq = min(1024, S)
    T = S + C
    tk = next(t for t in (3328, 1280, 640, 256, 128) if T % t == 0)
    return dict(tm=tm, tq=tq, tk=tk, tc=min(256, C), bm=512)


def kernel(x, c, ctx, c_ctx, ada_w, ada_b, norm1_g, norm2_g, w_in, na_qn, na_kn, na_rpb, gqa_qn, gqa_kn, mla_qa_g, mla_kva_g, mla_wuq, mla_wukv, mla_qn, mla_kn, lru_conv_w, lru_conv_b, lru_wa, lru_ba, lru_wi, lru_bi, lru_lam, grp_g, w_out, router_w, router_b, exp_w_gu, exp_b_gu, exp_w_down, exp_b_down):
    p = dict(norm1_g=norm1_g, norm2_g=norm2_g, w_in=w_in, na_qn=na_qn, na_kn=na_kn, na_rpb=na_rpb,
             gqa_qn=gqa_qn, gqa_kn=gqa_kn, mla_qa_g=mla_qa_g, mla_kva_g=mla_kva_g,
             mla_wuq=mla_wuq, mla_wukv=mla_wukv, mla_qn=mla_qn, mla_kn=mla_kn,
             lru_conv_w=lru_conv_w, lru_conv_b=lru_conv_b, lru_wa=lru_wa, lru_ba=lru_ba,
             lru_wi=lru_wi, lru_bi=lru_bi, lru_lam=lru_lam, grp_g=grp_g, w_out=w_out,
             router_w=router_w, router_b=router_b, exp_w_gu=exp_w_gu, exp_b_gu=exp_b_gu,
             exp_w_down=exp_w_down, exp_b_down=exp_b_down)
    B, S, D = x.shape
    assert B == 1 and S % GRID_W == 0
    C = ctx.shape[1]
    T = S + C
    L = ada_w.shape[0]
    ts = _tiles(S, C)
    tm, bm = ts["tm"], ts["bm"]
    consts = _constants(S, T, tm)

    c8 = jnp.zeros((8, D), F32).at[0].set(c[0]).at[1].set(c_ctx)
    mods = _modulation(c8, ada_w, ada_b)[:, :2].reshape(L, 2, 6, D)

    n_rows = -(-(T * TOP_K + (T // tm) * N_EXPERTS * (RUN_ALIGN - 1) + N_EXPERTS * (bm - 1)) // bm) * bm
    na_tabs = _na_bias_tables(na_rpb)
    n_exp = exp_w_gu.shape[1]
    w_gu, w_down = exp_w_gu, exp_w_down
    b_gu, b_down = exp_b_gu.reshape(L, n_exp, 1, -1), exp_b_down.reshape(L, n_exp, 1, -1)
    tq, tk = ts["tq"], ts["tk"]
    xs = jnp.concatenate([x[0], ctx[0]], axis=0)
    for l in range(L):
        lw = _layer_weights(l, p)
        mod = mods[l]
        (naq, nak, nav, gq, gk, gv, mq, mk, mv, lx, lg) = _premix(xs, mod, S, tm, consts, lw)
        oa = _neighbourhood(naq, nak, nav, na_tabs[l], S)
        ob_lat = _flash(gq, gk, gv, tq, tk, S // tq, 0, T // tk, 0)
        ob_ctx = _flash(gq, gk, gv, C, C, 1, S // C, 1, S // C)
        oc_lat = _flash(mq, mk, mv, tq, tk, S // tq, 0, T // tk, 0)
        oc_ctx = _flash(mq, mk, mv, C, C, 1, S // C, 1, S // C)
        yf = _lru_scan(lx, lw, S, ts["tc"], False)
        yb = _lru_scan(lx, lw, S, ts["tc"], True)
        x1, tok, slot, gate, cnt = _merge(xs, mod, oa, ob_lat, ob_ctx, oc_lat, oc_ctx, yf, yb, lg,
                                          S, tm, consts, lw)
        meta, zmeta, blk_expert, n_used = _routing_meta(cnt, bm, n_rows)
        xin = _dispatch(tok, slot, meta, zmeta, n_rows, tm, bm)
        y = _experts(xin, blk_expert, n_used, l, w_gu, b_gu, w_down, b_down, bm)
        xs = _combine(y, slot, gate, meta, x1, mod, S, tm)
    return xs[:S][None]
```
